```python
import math
import jax, jax.numpy as jnp
from jax import lax
import numpy as np

D_MODEL = 1024
BATCH = 32
SEQ = 2048
DEPTH = 4
DEC_BATCH = 4
DEC_SEQ = 4096
PAST_LEN = 128

SSM_WIDTH = D_MODEL // 4
SSM_GROUP = 16
SSM_GROUPS = SSM_WIDTH // SSM_GROUP
SSM_STATE = 64
LOG_DT_MIN = math.log(1e-3)
LOG_DT_MAX = math.log(1e-1)
POOL_WIDTH = D_MODEL // 4
POOL_WINDOWS = (2, 4, 8, 16)
POOL_GROUP = POOL_WIDTH // len(POOL_WINDOWS)
MLA_WIDTH = D_MODEL // 2
MLA_HEADS = 8
MLA_V_DIM = MLA_WIDTH // MLA_HEADS
MLA_NOPE_DIM = 64
MLA_ROPE_DIM = 32
MLA_QK_DIM = MLA_NOPE_DIM + MLA_ROPE_DIM
MLA_Q_LORA = D_MODEL // 4
MLA_KV_LORA = D_MODEL // 8
MLA_SCALE = MLA_QK_DIM ** -0.5
ROPE_BASE = 10000.0
Q_BLOCK = 128
MIX_WIDTH = SSM_WIDTH + POOL_WIDTH + MLA_WIDTH
IN_COLS = SSM_WIDTH + POOL_WIDTH + MLA_Q_LORA + MLA_KV_LORA + MLA_ROPE_DIM
D_FF = 4 * D_MODEL
NORM_EPS = 1e-6
RES_SCALE = (2 * DEPTH) ** -0.5

kernel_name = 'hymba_s5_pool_mla_encoder'


def rmsnorm(x, g):
    xf = x.astype(jnp.float32)
    xf = xf * lax.rsqrt(jnp.mean(xf * xf, axis=-1, keepdims=True) + NORM_EPS)
    return (xf * g.astype(jnp.float32)).astype(x.dtype)


def _complex_affine_combine(e1, e2):
    a1r, a1i, b1r, b1i = e1
    a2r, a2i, b2r, b2i = e2
    return (a2r * a1r - a2i * a1i,
            a2r * a1i + a2i * a1r,
            a2r * b1r - a2i * b1i + b2r,
            a2r * b1i + a2i * b1r + b2i)


def s5_mixer(u, a_re, a_im, log_dt, b_re, b_im, c_re, c_im, d_skip, glu_w, glu_b):
    bsz, seq, _ = u.shape
    uf = u.astype(jnp.float32)
    ug = uf.reshape(bsz, seq, SSM_GROUPS, SSM_GROUP)
    bu_re = jnp.einsum('blgh,gph->blgp', ug, b_re.astype(jnp.float32))
    bu_im = jnp.einsum('blgh,gph->blgp', ug, b_im.astype(jnp.float32))
    state_re = jnp.zeros_like(bu_re)
    state_im = jnp.zeros_like(bu_im)
    for direction in range(2):
        lam_re = a_re[direction].astype(jnp.float32)
        lam_im = a_im[direction].astype(jnp.float32)
        dt = jnp.exp(log_dt[direction].astype(jnp.float32))[:, None]
        mag = jnp.exp(lam_re * dt)
        ang = lam_im * dt
        lb_re = mag * jnp.cos(ang)
        lb_im = mag * jnp.sin(ang)
        den = lam_re * lam_re + lam_im * lam_im
        f_re = ((lb_re - 1.0) * lam_re + lb_im * lam_im) / den
        f_im = (lb_im * lam_re - (lb_re - 1.0) * lam_im) / den
        in_re = f_re * bu_re - f_im * bu_im
        in_im = f_re * bu_im + f_im * bu_re
        full = in_re.shape
        _, _, s_re, s_im = lax.associative_scan(
            _complex_affine_combine,
            (jnp.broadcast_to(lb_re, full), jnp.broadcast_to(lb_im, full), in_re, in_im),
            reverse=(direction == 1), axis=1)
        state_re = state_re + s_re
        state_im = state_im + s_im
    y = (jnp.einsum('blgp,ghp->blgh', state_re, c_re.astype(jnp.float32))
         - jnp.einsum('blgp,ghp->blgh', state_im, c_im.astype(jnp.float32)))
    y = y.reshape(bsz, seq, SSM_WIDTH) + d_skip.astype(jnp.float32) * uf
    y = jax.nn.gelu(y)
    y = y * jax.nn.sigmoid(y @ glu_w.astype(jnp.float32) + glu_b.astype(jnp.float32))
    return y.astype(u.dtype)


def pool_mixer(p, pool_w, pool_scale):
    bsz, seq, _ = p.shape
    pf = p.astype(jnp.float32)
    csum = jnp.concatenate([jnp.zeros((bsz, 1, POOL_WIDTH), jnp.float32),
                            jnp.cumsum(pf, axis=1)], axis=1)
    pos = jnp.arange(seq)
    outs = []
    for g, w in enumerate(POOL_WINDOWS):
        lo = jnp.clip(pos - w // 2, 0, seq)
        hi = jnp.clip(pos - w // 2 + w, 0, seq)
        cg = csum[..., g * POOL_GROUP:(g + 1) * POOL_GROUP]
        count = (hi - lo).astype(jnp.float32)[:, None]
        mean = (jnp.take(cg, hi, axis=1) - jnp.take(cg, lo, axis=1)) / count
        outs.append(mean - pf[..., g * POOL_GROUP:(g + 1) * POOL_GROUP])
    pooled = jnp.stack(outs, axis=2).astype(p.dtype)
    mixed = jnp.einsum('blgc,gcd->blgd', pooled, pool_w).reshape(bsz, seq, POOL_WIDTH)
    return mixed * pool_scale


def rope_tables(seq):
    inv_freq = ROPE_BASE ** (-jnp.arange(0, MLA_ROPE_DIM, 2, dtype=jnp.float32) / MLA_ROPE_DIM)
    ang = jnp.arange(seq, dtype=jnp.float32)[:, None] * inv_freq[None, :]
    return jnp.cos(ang), jnp.sin(ang)


def apply_rope(x, cos, sin):
    half = MLA_ROPE_DIM // 2
    cos = cos.astype(x.dtype)
    sin = sin.astype(x.dtype)
    x1, x2 = x[..., :half], x[..., half:]
    return jnp.concatenate([x1 * cos - x2 * sin, x1 * sin + x2 * cos], axis=-1)


def mla_mixer(c_q, c_kv, k_rope, q_norm_g, w_uq, kv_norm_g, w_ukv):
    bsz, seq, _ = c_q.shape
    q = (rmsnorm(c_q, q_norm_g) @ w_uq).reshape(bsz, seq, MLA_HEADS, MLA_QK_DIM)
    q_nope, q_rope = q[..., :MLA_NOPE_DIM], q[..., MLA_NOPE_DIM:]
    kv = (rmsnorm(c_kv, kv_norm_g) @ w_ukv).reshape(bsz, seq, MLA_HEADS, MLA_NOPE_DIM + MLA_V_DIM)
    k_nope, v = kv[..., :MLA_NOPE_DIM], kv[..., MLA_NOPE_DIM:]
    cos, sin = rope_tables(seq)
    q_rope = apply_rope(q_rope, cos[:, None, :], sin[:, None, :])
    k_rope = apply_rope(k_rope, cos, sin)
    n_blk = seq // Q_BLOCK
    qn_blocks = q_nope.reshape(bsz, n_blk, Q_BLOCK, MLA_HEADS, MLA_NOPE_DIM).transpose(1, 0, 2, 3, 4)
    qr_blocks = q_rope.reshape(bsz, n_blk, Q_BLOCK, MLA_HEADS, MLA_ROPE_DIM).transpose(1, 0, 2, 3, 4)

    def attend(blk):
        qn, qr = blk
        s = (jnp.einsum('bqhd,bkhd->bhqk', qn, k_nope)
             + jnp.einsum('bqhr,bkr->bhqk', qr, k_rope))
        prob = jax.nn.softmax(s.astype(jnp.float32) * MLA_SCALE, axis=-1).astype(v.dtype)
        return jnp.einsum('bhqk,bkhd->bqhd', prob, v)

    o = lax.map(attend, (qn_blocks, qr_blocks))
    return o.transpose(1, 0, 2, 3, 4).reshape(bsz, seq, MLA_WIDTH)


def hybrid_layer(x, norm_mix_g, w_in, ssm_a_re, ssm_a_im, ssm_log_dt, ssm_b_re, ssm_b_im,
                 ssm_c_re, ssm_c_im, ssm_d, ssm_glu_w, ssm_glu_b, pool_w, pool_scale,
                 mla_q_norm_g, mla_w_uq, mla_kv_norm_g, mla_w_ukv, w_out,
                 norm_mlp_g, mlp_w_up, mlp_w_down):
    h = rmsnorm(x, norm_mix_g)
    proj = h @ w_in
    o0 = SSM_WIDTH
    o1 = o0 + POOL_WIDTH
    o2 = o1 + MLA_Q_LORA
    o3 = o2 + MLA_KV_LORA
    y_ssm = s5_mixer(proj[..., :o0], ssm_a_re, ssm_a_im, ssm_log_dt, ssm_b_re, ssm_b_im,
                     ssm_c_re, ssm_c_im, ssm_d, ssm_glu_w, ssm_glu_b)
    y_pool = pool_mixer(proj[..., o0:o1], pool_w, pool_scale)
    y_mla = mla_mixer(proj[..., o1:o2], proj[..., o2:o3], proj[..., o3:],
                      mla_q_norm_g, mla_w_uq, mla_kv_norm_g, mla_w_ukv)
    x = x + jnp.concatenate([y_ssm, y_pool, y_mla], axis=-1) @ w_out
    hm = rmsnorm(x, norm_mlp_g) @ mlp_w_up
    hm = jnp.square(jax.nn.relu(hm))
    return x + hm @ mlp_w_down


def trunk(x, norm_mix_g, w_in, ssm_a_re, ssm_a_im, ssm_log_dt, ssm_b_re, ssm_b_im,
          ssm_c_re, ssm_c_im, ssm_d, ssm_glu_w, ssm_glu_b, pool_w, pool_scale,
          mla_q_norm_g, mla_w_uq, mla_kv_norm_g, mla_w_ukv, w_out,
          norm_mlp_g, mlp_w_up, mlp_w_down, final_norm_g):
    for l in range(DEPTH):
        x = hybrid_layer(x, norm_mix_g[l], w_in[l], ssm_a_re[l], ssm_a_im[l], ssm_log_dt[l],
                         ssm_b_re[l], ssm_b_im[l], ssm_c_re[l], ssm_c_im[l], ssm_d[l],
                         ssm_glu_w[l], ssm_glu_b[l], pool_w[l], pool_scale[l],
                         mla_q_norm_g[l], mla_w_uq[l], mla_kv_norm_g[l], mla_w_ukv[l], w_out[l],
                         norm_mlp_g[l], mlp_w_up[l], mlp_w_down[l])
    return rmsnorm(x, final_norm_g)


def setup_inputs(seed: int = 0) -> dict:
    key = jax.random.key(seed)
    ks = jax.random.split(key, 24)
    f32 = jnp.float32

    def nrm(k, shape, scale):
        return scale * jax.random.normal(k, shape, f32)

    def gain(k, shape):
        return 1.0 + 0.02 * jax.random.normal(k, shape, f32)

    a_shape = (DEPTH, 2, SSM_GROUPS, SSM_STATE)
    n_idx = jnp.arange(SSM_STATE, dtype=f32)
    return {
        'x_prompt': jax.random.normal(ks[0], (BATCH, SEQ, D_MODEL), f32),
        'x_sample': jax.random.normal(ks[1], (DEC_BATCH, DEC_SEQ, D_MODEL), f32),
        'norm_mix_g': gain(ks[2], (DEPTH, D_MODEL)),
        'w_in': nrm(ks[3], (DEPTH, D_MODEL, IN_COLS), D_MODEL ** -0.5),
        'ssm_a_re': -0.5 + nrm(ks[4], a_shape, 0.01),
        'ssm_a_im': math.pi * n_idx + nrm(ks[5], a_shape, 0.01),
        'ssm_log_dt': jax.random.uniform(ks[6], (DEPTH, 2, SSM_GROUPS), f32, LOG_DT_MIN, LOG_DT_MAX),
        'ssm_b_re': nrm(ks[7], (DEPTH, SSM_GROUPS, SSM_STATE, SSM_GROUP), (2 * SSM_GROUP) ** -0.5),
        'ssm_b_im': nrm(ks[8], (DEPTH, SSM_GROUPS, SSM_STATE, SSM_GROUP), (2 * SSM_GROUP) ** -0.5),
        'ssm_c_re': nrm(ks[9], (DEPTH, SSM_GROUPS, SSM_GROUP, SSM_STATE), 0.5),
        'ssm_c_im': nrm(ks[10], (DEPTH, SSM_GROUPS, SSM_GROUP, SSM_STATE), 0.5),
        'ssm_d': nrm(ks[11], (DEPTH, SSM_WIDTH), 1.0),
        'ssm_glu_w': nrm(ks[12], (DEPTH, SSM_WIDTH, SSM_WIDTH), SSM_WIDTH ** -0.5),
        'ssm_glu_b': nrm(ks[13], (DEPTH, SSM_WIDTH), 0.01),
        'pool_w': nrm(ks[14], (DEPTH, len(POOL_WINDOWS), POOL_GROUP, POOL_GROUP), POOL_GROUP ** -0.5),
        'pool_scale': gain(ks[15], (DEPTH, POOL_WIDTH)),
        'mla_q_norm_g': gain(ks[16], (DEPTH, MLA_Q_LORA)),
        'mla_w_uq': nrm(ks[17], (DEPTH, MLA_Q_LORA, MLA_HEADS * MLA_QK_DIM), MLA_Q_LORA ** -0.5),
        'mla_kv_norm_g': gain(ks[18], (DEPTH, MLA_KV_LORA)),
        'mla_w_ukv': nrm(ks[19], (DEPTH, MLA_KV_LORA, MLA_HEADS * (MLA_NOPE_DIM + MLA_V_DIM)), MLA_KV_LORA ** -0.5),
        'w_out': nrm(ks[20], (DEPTH, MIX_WIDTH, D_MODEL), MIX_WIDTH ** -0.5 * RES_SCALE),
        'norm_mlp_g': gain(ks[21], (DEPTH, D_MODEL)),
        'mlp_w_up': nrm(ks[22], (DEPTH, D_MODEL, D_FF), D_MODEL ** -0.5),
        'mlp_w_down': nrm(ks[23], (DEPTH, D_FF, D_MODEL), D_FF ** -0.5 * RES_SCALE),
        'final_norm_g': gain(jax.random.fold_in(key, 99), (D_MODEL,)),
    }


def reference(x_prompt, x_sample, norm_mix_g, w_in, ssm_a_re, ssm_a_im, ssm_log_dt,
              ssm_b_re, ssm_b_im, ssm_c_re, ssm_c_im, ssm_d, ssm_glu_w, ssm_glu_b,
              pool_w, pool_scale, mla_q_norm_g, mla_w_uq, mla_kv_norm_g, mla_w_ukv,
              w_out, norm_mlp_g, mlp_w_up, mlp_w_down, final_norm_g):
    y_prompt = trunk(x_prompt, norm_mix_g, w_in, ssm_a_re, ssm_a_im, ssm_log_dt,
                     ssm_b_re, ssm_b_im, ssm_c_re, ssm_c_im, ssm_d, ssm_glu_w, ssm_glu_b,
                     pool_w, pool_scale, mla_q_norm_g, mla_w_uq, mla_kv_norm_g, mla_w_ukv,
                     w_out, norm_mlp_g, mlp_w_up, mlp_w_down, final_norm_g)
    y_sample = trunk(x_sample, norm_mix_g, w_in, ssm_a_re, ssm_a_im, ssm_log_dt,
                     ssm_b_re, ssm_b_im, ssm_c_re, ssm_c_im, ssm_d, ssm_glu_w, ssm_glu_b,
                     pool_w, pool_scale, mla_q_norm_g, mla_w_uq, mla_kv_norm_g, mla_w_ukv,
                     w_out, norm_mlp_g, mlp_w_up, mlp_w_down, final_norm_g)
    return (y_prompt, y_sample)
```

```python
import functools
import math

import jax
import jax.numpy as jnp
from jax import lax
from jax.experimental import pallas as pl
from jax.experimental.pallas import tpu as pltpu

F32 = jnp.float32
BF16 = jnp.bfloat16

D_MODEL = 1024
SSM_WIDTH = 256
SSM_GROUP = 16
SSM_GROUPS = 16
SSM_STATE = 64
POOL_WIDTH = 256
POOL_WINDOWS = (2, 4, 8, 16)
POOL_GROUP = 64
MLA_HEADS = 8
MLA_V_DIM = 64
MLA_NOPE_DIM = 64
MLA_ROPE_DIM = 32
MLA_QK_DIM = MLA_NOPE_DIM + MLA_ROPE_DIM
MLA_Q_LORA = 256
MLA_KV_LORA = 128
MLA_WIDTH = MLA_HEADS * MLA_V_DIM
MLA_SCALE = MLA_QK_DIM ** -0.5
ROPE_BASE = 10000.0
D_FF = 4 * D_MODEL
NORM_EPS = 1e-6

LANES = 128
SUBLANES = 8
MXU_DIM = 256
VMEM_BYTES = 64 * 1024 * 1024

HEAD_PAD = LANES
QK_PAD = MLA_HEADS * HEAD_PAD
S5_CHUNK = MXU_DIM // SSM_GROUP
S5_PAIR = 2 * SSM_STATE
S5_PAIRS = SSM_GROUPS // 2
S5_PAIR_COLS = 2 * MXU_DIM
POOL_HALO = max(POOL_WINDOWS)
IN_COLS_PAD = SSM_WIDTH + POOL_WIDTH + MLA_Q_LORA + MLA_KV_LORA + HEAD_PAD
FF_CHUNK = 1024


def _vmem_limit(block_bytes):
    return int(min(VMEM_BYTES - 8 * 1024 * 1024, block_bytes))


def _rms(x, g):
    return x * lax.rsqrt(jnp.mean(x * x, axis=-1, keepdims=True) + NORM_EPS) * g


def _const_spec(shape):
    nd = len(shape)
    return pl.BlockSpec(shape, lambda *_: (0,) * nd, pipeline_mode=pl.Buffered(1))


def _inproj_kernel(x_ref, g_ref, win_ref, qg_ref, wuq_ref, kvg_ref, wuk_ref, wuv_ref,
                   cos_ref, sa_ref, sb_ref, u_ref, p_ref, q_ref, k_ref, v_ref):
    h = _rms(x_ref[...], g_ref[...]).astype(BF16)
    proj = jnp.dot(h, win_ref[...], preferred_element_type=F32)
    o0 = SSM_WIDTH
    o1 = o0 + POOL_WIDTH
    o2 = o1 + MLA_Q_LORA
    o3 = o2 + MLA_KV_LORA
    u_ref[...] = proj[:, :o0]
    p_ref[...] = proj[:, o0:o1]
    cos = cos_ref[...]
    sa = sa_ref[...]
    sb = sb_ref[...]

    def rope(t):
        return (t * cos + pltpu.roll(t, HEAD_PAD - MLA_ROPE_DIM // 2, 1) * sa
                + pltpu.roll(t, MLA_ROPE_DIM // 2, 1) * sb)

    qn = _rms(proj[:, o1:o2], qg_ref[...]).astype(BF16)
    q = jnp.dot(qn, wuq_ref[...], preferred_element_type=F32)
    for hd in range(MLA_HEADS):
        sl = slice(hd * HEAD_PAD, (hd + 1) * HEAD_PAD)
        q_ref[:, sl] = rope(q[:, sl]).astype(BF16)
    kn = _rms(proj[:, o2:o3], kvg_ref[...]).astype(BF16)
    kk = jnp.dot(kn, wuk_ref[...], preferred_element_type=F32)
    kr = rope(proj[:, o3:])
    for hd in range(MLA_HEADS):
        sl = slice(hd * HEAD_PAD, (hd + 1) * HEAD_PAD)
        k_ref[:, sl] = (kk[:, sl] + kr).astype(BF16)
    v_ref[...] = jnp.dot(kn, wuv_ref[...], preferred_element_type=F32).astype(BF16)


def _inproj(x2d, seq, g, win, qg, wuq, kvg, wuk, wuv, cos_t, sa_t, sb_t, *, tm):
    n = x2d.shape[0]
    nlb = seq // tm
    row = lambda i: (i, 0)
    tab = lambda i: (lax.rem(i, nlb), 0)
    out_shapes = (
        jax.ShapeDtypeStruct((n, SSM_WIDTH), F32),
        jax.ShapeDtypeStruct((n, POOL_WIDTH), F32),
        jax.ShapeDtypeStruct((n, QK_PAD), BF16),
        jax.ShapeDtypeStruct((n, QK_PAD), BF16),
        jax.ShapeDtypeStruct((n, MLA_WIDTH), BF16),
    )
    vmem = (2 * tm * D_MODEL * 4 + 2 * tm * (2 * 256 * 4 + 2 * QK_PAD * 2 + MLA_WIDTH * 2)
            + 2 * (win.size + wuq.size + wuk.size + wuv.size) + 6 * tm * HEAD_PAD * 4
            + 8 * tm * D_MODEL * 4)
    return pl.pallas_call(
        _inproj_kernel,
        grid=(n // tm,),
        in_specs=[
            pl.BlockSpec((tm, D_MODEL), row),
            _const_spec(g.shape), _const_spec(win.shape), _const_spec(qg.shape),
            _const_spec(wuq.shape), _const_spec(kvg.shape), _const_spec(wuk.shape),
            _const_spec(wuv.shape),
            pl.BlockSpec((tm, HEAD_PAD), tab), pl.BlockSpec((tm, HEAD_PAD), tab),
            pl.BlockSpec((tm, HEAD_PAD), tab),
        ],
        out_specs=(
            pl.BlockSpec((tm, SSM_WIDTH), row), pl.BlockSpec((tm, POOL_WIDTH), row),
            pl.BlockSpec((tm, QK_PAD), row), pl.BlockSpec((tm, QK_PAD), row),
            pl.BlockSpec((tm, MLA_WIDTH), row),
        ),
        out_shape=out_shapes,
        compiler_params=pltpu.CompilerParams(
            dimension_semantics=("parallel",), vmem_limit_bytes=_vmem_limit(vmem)),
        name="inproj",
    )(x2d, g, win, qg, wuq, kvg, wuk, wuv, cos_t, sa_t, sb_t)


def _s5_kernel(u_ref, kt_ref, winj_ref, wout_ref, lam_ref, y_ref, inj_ref, st_ref, *, chunks):
    u = u_ref[...]
    inj_ref[...] = jnp.dot(u, winj_ref[...], preferred_element_type=F32)
    lam = lam_ref[...]
    lf_re = jnp.broadcast_to(lam[0:1], (SUBLANES, S5_PAIR))
    lf_im = jnp.broadcast_to(lam[1:2], (SUBLANES, S5_PAIR))
    lb_re = jnp.broadcast_to(lam[2:3], (SUBLANES, S5_PAIR))
    lb_im = jnp.broadcast_to(lam[3:4], (SUBLANES, S5_PAIR))
    zero = jnp.zeros((SUBLANES, S5_PAIR), F32)

    def step(k, carry):
        xf_re, xf_im, xb_re, xb_im = carry
        rf = pl.multiple_of(k * SUBLANES, SUBLANES)
        rb = pl.multiple_of((chunks - 1 - k) * SUBLANES, SUBLANES)
        st_ref[pl.ds(rf, SUBLANES), 0:S5_PAIR] = xf_re
        st_ref[pl.ds(rf, SUBLANES), S5_PAIR:2 * S5_PAIR] = xf_im
        st_ref[pl.ds(rb, SUBLANES), 2 * S5_PAIR:3 * S5_PAIR] = xb_re
        st_ref[pl.ds(rb, SUBLANES), 3 * S5_PAIR:4 * S5_PAIR] = xb_im
        if_re = inj_ref[pl.ds(rf, SUBLANES), 0:S5_PAIR]
        if_im = inj_ref[pl.ds(rf, SUBLANES), S5_PAIR:2 * S5_PAIR]
        ib_re = inj_ref[pl.ds(rb, SUBLANES), 2 * S5_PAIR:3 * S5_PAIR]
        ib_im = inj_ref[pl.ds(rb, SUBLANES), 3 * S5_PAIR:4 * S5_PAIR]
        nf_re = lf_re * xf_re - lf_im * xf_im + if_re
        nf_im = lf_re * xf_im + lf_im * xf_re + if_im
        nb_re = lb_re * xb_re - lb_im * xb_im + ib_re
        nb_im = lb_re * xb_im + lb_im * xb_re + ib_im
        return nf_re, nf_im, nb_re, nb_im

    lax.fori_loop(0, chunks, step, (zero, zero, zero, zero))
    y_carry = jnp.dot(st_ref[...].astype(BF16), wout_ref[...], preferred_element_type=F32)
    y0 = jnp.dot(u[:, :MXU_DIM], kt_ref[0], preferred_element_type=F32)
    y1 = jnp.dot(u[:, MXU_DIM:], kt_ref[1], preferred_element_type=F32)
    y_ref[:, :MXU_DIM] = y0 + y_carry[:, :MXU_DIM]
    y_ref[:, MXU_DIM:] = y1 + y_carry[:, MXU_DIM:]


def _s5(u_t, kt, winj, wout, lam16):
    npairs, nbg, rows, cols = u_t.shape
    chunks = rows // SUBLANES
    vmem = (2 * rows * cols * 2 + 2 * rows * cols * 4 + 2 * rows * cols * 4
            + 2 * 2 * (2 * MXU_DIM * MXU_DIM + 2 * cols * cols) + 6 * rows * cols * 4)
    return pl.pallas_call(
        functools.partial(_s5_kernel, chunks=chunks),
        grid=(npairs, nbg),
        in_specs=[
            pl.BlockSpec((None, None, rows, cols), lambda gp, bg: (gp, bg, 0, 0)),
            pl.BlockSpec((None, 2, MXU_DIM, MXU_DIM), lambda gp, bg: (gp, 0, 0, 0)),
            pl.BlockSpec((None, cols, cols), lambda gp, bg: (gp, 0, 0)),
            pl.BlockSpec((None, cols, cols), lambda gp, bg: (gp, 0, 0)),
            pl.BlockSpec((None, 4, S5_PAIR), lambda gp, bg: (gp, 0, 0)),
        ],
        out_specs=pl.BlockSpec((None, None, rows, cols), lambda gp, bg: (gp, bg, 0, 0)),
        out_shape=jax.ShapeDtypeStruct((npairs, nbg, rows, cols), F32),
        scratch_shapes=[pltpu.VMEM((rows, cols), F32), pltpu.VMEM((rows, cols), F32)],
        compiler_params=pltpu.CompilerParams(
            dimension_semantics=("parallel", "parallel"), vmem_limit_bytes=_vmem_limit(vmem)),
        name="s5_mixer",
    )(u_t, kt, winj, wout, lam16)


def _pool_kernel(p_ref, w_ref, sc_ref, o_ref, pad_ref, *, seq, tile):
    width = POOL_WIDTH
    pad_ref[0:POOL_HALO, :] = jnp.zeros((POOL_HALO, width), F32)
    pad_ref[POOL_HALO + seq:POOL_HALO + seq + POOL_HALO, :] = jnp.zeros((POOL_HALO, width), F32)
    pad_ref[POOL_HALO:POOL_HALO + seq, :] = p_ref[...]
    lane = lax.broadcasted_iota(jnp.int32, (tile, LANES), 1)
    left = lane < POOL_GROUP
    row = lax.broadcasted_iota(jnp.int32, (tile, LANES), 0)
    hw_a = jnp.where(left, POOL_WINDOWS[0] // 2, POOL_WINDOWS[1] // 2)
    hw_b = jnp.where(left, POOL_WINDOWS[2] // 2, POOL_WINDOWS[3] // 2)
    w = w_ref[...]
    sc = sc_ref[...]
    for t0 in range(0, seq, tile):
        halves = []
        for half, hw in ((0, hw_a), (1, hw_b)):
            cols = slice(half * LANES, (half + 1) * LANES)
            wmax = POOL_WINDOWS[2 * half + 1]
            wmin = POOL_WINDOWS[2 * half]
            acc = jnp.zeros((tile, LANES), F32)
            for off in range(-(wmax // 2), wmax // 2):
                xs = pad_ref[POOL_HALO + t0 + off:POOL_HALO + t0 + off + tile, cols]
                if -(wmin // 2) <= off < wmin // 2:
                    acc = acc + xs
                else:
                    acc = acc + jnp.where(left, 0.0, xs)
            pos = row + t0
            lo = jnp.maximum(pos - hw, 0)
            hi = jnp.minimum(pos + hw, seq)
            cnt = (hi - lo).astype(F32)
            centre = pad_ref[POOL_HALO + t0:POOL_HALO + t0 + tile, cols]
            halves.append(acc / cnt - centre)
        pooled = jnp.concatenate(halves, axis=-1).astype(BF16)
        mixed = jnp.dot(pooled, w, preferred_element_type=F32) * sc
        o_ref[t0:t0 + tile, :] = mixed.astype(BF16)


def _pool(p2d, seq, w_bd, scale, *, tile):
    n = p2d.shape[0]
    vmem = 2 * seq * POOL_WIDTH * 4 + 2 * seq * POOL_WIDTH * 2 + (seq + 32) * POOL_WIDTH * 4 \
        + 64 * tile * LANES * 4 + 4 * POOL_WIDTH * POOL_WIDTH
    return pl.pallas_call(
        functools.partial(_pool_kernel, seq=seq, tile=tile),
        grid=(n // seq,),
        in_specs=[pl.BlockSpec((seq, POOL_WIDTH), lambda b: (b, 0)),
                  _const_spec(w_bd.shape), _const_spec(scale.shape)],
        out_specs=pl.BlockSpec((seq, POOL_WIDTH), lambda b: (b, 0)),
        out_shape=jax.ShapeDtypeStruct((n, POOL_WIDTH), BF16),
        scratch_shapes=[pltpu.VMEM((seq + 2 * POOL_HALO, POOL_WIDTH), F32)],
        compiler_params=pltpu.CompilerParams(
            dimension_semantics=("parallel",), vmem_limit_bytes=_vmem_limit(vmem)),
        name="pool_mixer",
    )(p2d, w_bd, scale)


def _attn_kernel(q_ref, k_ref, v_ref, o_ref):
    for hh in range(2):
        q = q_ref[:, hh * HEAD_PAD:(hh + 1) * HEAD_PAD]
        k = k_ref[:, hh * HEAD_PAD:(hh + 1) * HEAD_PAD]
        s = lax.dot_general(q, k, (((1,), (1,)), ((), ())), preferred_element_type=F32)
        z = s * MLA_SCALE
        m = jnp.max(z, axis=-1, keepdims=True)
        e = jnp.exp(z - m)
        denom = jnp.sum(e, axis=-1, keepdims=True)
        o = jnp.dot(e.astype(BF16), v_ref[:, hh * MLA_V_DIM:(hh + 1) * MLA_V_DIM],
                    preferred_element_type=F32)
        o_ref[:, hh * MLA_V_DIM:(hh + 1) * MLA_V_DIM] = (o / denom).astype(BF16)


def _attention(q, k, v, seq, *, tq):
    n = q.shape[0]
    nb = n // seq
    nq = seq // tq
    pairs = MLA_HEADS // 2
    vmem = (2 * tq * 2 * HEAD_PAD * 2 + 2 * seq * 2 * HEAD_PAD * 2 + 2 * seq * LANES * 2
            + 2 * tq * LANES * 2 + 4 * tq * seq * 4)
    return pl.pallas_call(
        _attn_kernel,
        grid=(nb, pairs, nq),
        in_specs=[
            pl.BlockSpec((tq, 2 * HEAD_PAD), lambda b, hp, i: (b * nq + i, hp)),
            pl.BlockSpec((seq, 2 * HEAD_PAD), lambda b, hp, i: (b, hp)),
            pl.BlockSpec((seq, 2 * MLA_V_DIM), lambda b, hp, i: (b, hp)),
        ],
        out_specs=pl.BlockSpec((tq, 2 * MLA_V_DIM), lambda b, hp, i: (b * nq + i, hp)),
        out_shape=jax.ShapeDtypeStruct((n, MLA_WIDTH), BF16),
        compiler_params=pltpu.CompilerParams(
            dimension_semantics=("parallel", "parallel", "parallel"),
            vmem_limit_bytes=_vmem_limit(vmem)),
        name="mla_attention",
    )(q, k, v)


def _outmlp_kernel(x_ref, u_ref, yc_ref, yp_ref, ym_ref, d_ref, gw_ref, gb_ref, wo_ref,
                   g_ref, up_ref, dn_ref, fg_ref, o_ref, *, final):
    y = yc_ref[...] + d_ref[...] * u_ref[...]
    y = jax.nn.gelu(y)
    gate = jnp.dot(y.astype(BF16), gw_ref[...], preferred_element_type=F32) + gb_ref[...]
    y_ssm = (y * jax.nn.sigmoid(gate)).astype(BF16)
    o1 = SSM_WIDTH + POOL_WIDTH
    mix = jnp.dot(y_ssm, wo_ref[0:SSM_WIDTH, :], preferred_element_type=F32)
    mix = mix + jnp.dot(yp_ref[...], wo_ref[SSM_WIDTH:o1, :], preferred_element_type=F32)
    mix = mix + jnp.dot(ym_ref[...], wo_ref[o1:, :], preferred_element_type=F32)
    x1 = x_ref[...] + mix
    hn = _rms(x1, g_ref[...]).astype(BF16)
    acc = jnp.zeros_like(x1)
    for c in range(0, D_FF, FF_CHUNK):
        hc = jnp.dot(hn, up_ref[:, c:c + FF_CHUNK], preferred_element_type=F32)
        hc = jnp.square(jnp.maximum(hc, 0.0)).astype(BF16)
        acc = acc + jnp.dot(hc, dn_ref[c:c + FF_CHUNK, :], preferred_element_type=F32)
    x2 = x1 + acc
    if final:
        x2 = _rms(x2, fg_ref[...])
    o_ref[...] = x2


def _outmlp(x2d, u, yc, yp, ym, d, gw, gb, wo, g, up, dn, fg, *, tm, final):
    n = x2d.shape[0]
    row = lambda i: (i, 0)
    vmem = (4 * tm * D_MODEL * 4 + 2 * tm * (2 * 256 * 4 + 256 * 2 + MLA_WIDTH * 2)
            + 2 * (gw.size + wo.size + up.size + dn.size)
            + 6 * tm * D_MODEL * 4 + 3 * tm * FF_CHUNK * 4)
    return pl.pallas_call(
        functools.partial(_outmlp_kernel, final=final),
        grid=(n // tm,),
        in_specs=[
            pl.BlockSpec((tm, D_MODEL), row), pl.BlockSpec((tm, SSM_WIDTH), row),
            pl.BlockSpec((tm, SSM_WIDTH), row), pl.BlockSpec((tm, POOL_WIDTH), row),
            pl.BlockSpec((tm, MLA_WIDTH), row),
            _const_spec(d.shape), _const_spec(gw.shape), _const_spec(gb.shape),
            _const_spec(wo.shape), _const_spec(g.shape), _const_spec(up.shape),
            _const_spec(dn.shape), _const_spec(fg.shape),
        ],
        out_specs=pl.BlockSpec((tm, D_MODEL), row),
        out_shape=jax.ShapeDtypeStruct((n, D_MODEL), F32),
        compiler_params=pltpu.CompilerParams(
            dimension_semantics=("parallel",), vmem_limit_bytes=_vmem_limit(vmem)),
        name="outproj_mlp",
    )(x2d, u, yc, yp, ym, d, gw, gb, wo, g, up, dn, fg)


def _rope_tables(seq):
    half = MLA_ROPE_DIM // 2
    inv_freq = ROPE_BASE ** (-jnp.arange(0, MLA_ROPE_DIM, 2, dtype=F32) / MLA_ROPE_DIM)
    ang = jnp.arange(seq, dtype=F32)[:, None] * inv_freq[None, :]
    cos, sin = jnp.cos(ang), jnp.sin(ang)
    z = jnp.zeros_like(cos)
    ones = jnp.ones((seq, MLA_NOPE_DIM), F32)
    pad = jnp.zeros((seq, HEAD_PAD - MLA_QK_DIM), F32)
    zn = jnp.zeros((seq, MLA_NOPE_DIM), F32)
    cos_t = jnp.concatenate([ones, cos, cos, pad], axis=-1)
    sa_t = jnp.concatenate([zn, -sin, z, pad], axis=-1)
    sb_t = jnp.concatenate([zn, z, sin, pad], axis=-1)
    return cos_t, sa_t, sb_t


def _s5_matrices(a_re, a_im, log_dt, b_re, b_im, c_re, c_im):
    hp = lax.Precision.HIGHEST
    t = S5_CHUNK
    g, p, hch = SSM_GROUPS, SSM_STATE, SSM_GROUP
    dt = jnp.exp(log_dt)[..., None]
    mag = jnp.exp(a_re * dt)
    ang = a_im * dt
    lb_re = mag * jnp.cos(ang)
    lb_im = mag * jnp.sin(ang)
    den = a_re * a_re + a_im * a_im
    f_re = ((lb_re - 1.0) * a_re + lb_im * a_im) / den
    f_im = (lb_im * a_re - (lb_re - 1.0) * a_im) / den
    pw_re = [jnp.ones_like(lb_re)]
    pw_im = [jnp.zeros_like(lb_im)]
    for _ in range(t):
        r, i = pw_re[-1], pw_im[-1]
        pw_re.append(r * lb_re - i * lb_im)
        pw_im.append(r * lb_im + i * lb_re)
    pw_re = jnp.stack(pw_re)
    pw_im = jnp.stack(pw_im)
    bb_re = f_re[..., None] * b_re[None] - f_im[..., None] * b_im[None]
    bb_im = f_re[..., None] * b_im[None] + f_im[..., None] * b_re[None]
    w_re = pw_re[..., None] * bb_re[None] - pw_im[..., None] * bb_im[None]
    w_im = pw_re[..., None] * bb_im[None] + pw_im[..., None] * bb_re[None]
    m = (jnp.einsum('gkp,ndgph->ndghk', c_re, w_re, precision=hp)
         - jnp.einsum('gkp,ndgph->ndghk', c_im, w_im, precision=hp))
    jj = jnp.arange(t)[:, None]
    ii = jnp.arange(t)[None, :]
    lag_f = jnp.clip(ii - jj, 0, t - 1)
    lag_b = jnp.clip(jj - ii, 0, t - 1)
    kf = jnp.where((jj <= ii)[:, :, None, None, None], m[lag_f, 0], 0.0)
    kb = jnp.where((jj >= ii)[:, :, None, None, None], m[lag_b, 1], 0.0)
    kt = jnp.transpose(kf + kb, (2, 0, 3, 1, 4)).reshape(g, t * hch, t * hch)
    kt = kt.reshape(S5_PAIRS, 2, MXU_DIM, MXU_DIM)
    rev = jnp.arange(t - 1, -1, -1)
    parts = jnp.stack([w_re[rev, 0], w_im[rev, 0], w_re[:t, 1], w_im[:t, 1]])
    parts = jnp.transpose(parts, (0, 2, 1, 4, 3))
    parts = parts.reshape(4, S5_PAIRS, 2, t, hch, p)
    eye = jnp.eye(2, dtype=F32)
    winj = jnp.einsum('ab,kqajhp->qajhkbp', eye, parts).reshape(S5_PAIRS, S5_PAIR_COLS, 4 * S5_PAIR)
    idx_f = jnp.arange(1, t + 1)
    idx_b = jnp.arange(t, 0, -1)

    def c_times(pr, pi):
        cr = c_re[None] * jnp.transpose(pr, (0, 1, 2))[:, :, None, :] \
            - c_im[None] * pi[:, :, None, :]
        ci = c_re[None] * pi[:, :, None, :] + c_im[None] * pr[:, :, None, :]
        return cr, -ci

    of_re, of_im = c_times(pw_re[idx_f, 0], pw_im[idx_f, 0])
    ob_re, ob_im = c_times(pw_re[idx_b, 1], pw_im[idx_b, 1])
    outs = jnp.stack([of_re, of_im, ob_re, ob_im])
    outs = jnp.transpose(outs, (0, 2, 4, 1, 3)).reshape(4, S5_PAIRS, 2, p, t, hch)
    wout = jnp.einsum('ab,kqapih->kqapbih', eye, outs)
    wout = jnp.transpose(wout, (1, 0, 2, 3, 4, 5, 6)).reshape(S5_PAIRS, 4 * S5_PAIR, S5_PAIR_COLS)
    lam16 = jnp.stack([pw_re[t, 0], pw_im[t, 0], pw_re[t, 1], pw_im[t, 1]])
    lam16 = jnp.transpose(lam16.reshape(4, S5_PAIRS, S5_PAIR), (1, 0, 2))
    return kt.astype(BF16), winj.astype(BF16), wout.astype(BF16), lam16


def _to_chunk_rows(u2d, batch, seq):
    chunks = seq // S5_CHUNK
    u = u2d.reshape(batch, seq, SSM_WIDTH)
    bp = -(-batch // SUBLANES) * SUBLANES
    if bp != batch:
        u = jnp.concatenate([u, jnp.zeros((bp - batch, seq, SSM_WIDTH), u.dtype)], axis=0)
    u = u.reshape(bp // SUBLANES, SUBLANES, chunks, S5_CHUNK, S5_PAIRS, 2, SSM_GROUP)
    u = jnp.transpose(u, (4, 0, 2, 1, 5, 3, 6))
    return u.reshape(S5_PAIRS, bp // SUBLANES, chunks * SUBLANES, S5_PAIR_COLS).astype(BF16)


def _from_chunk_rows(y, batch, seq):
    chunks = seq // S5_CHUNK
    nbg = y.shape[1]
    y = y.reshape(S5_PAIRS, nbg, chunks, SUBLANES, 2, S5_CHUNK, SSM_GROUP)
    y = jnp.transpose(y, (1, 3, 2, 5, 0, 4, 6)).reshape(nbg * SUBLANES, seq, SSM_WIDTH)
    return y[:batch].reshape(batch * seq, SSM_WIDTH)


def _pad_cols(w, lo, hi):
    return jnp.pad(w, [(0, 0)] * (w.ndim - 1) + [(lo, hi)])


def _trunk(x, prm, depth):
    batch, seq, _ = x.shape
    n = batch * seq
    x2d = x.reshape(n, D_MODEL)
    cos_t, sa_t, sb_t = _rope_tables(seq)
    tm = min(512, seq)
    for l in range(depth):
        w = {k: v[l] for k, v in prm.items() if k != 'final_g'}
        u, p, q, k, v = _inproj(x2d, seq, w['g_mix'], w['win'], w['qg'], w['wuq'], w['kvg'],
                                w['wuk'], w['wuv'], cos_t, sa_t, sb_t, tm=tm)
        y_t = _s5(_to_chunk_rows(u, batch, seq), w['kt'], w['winj'], w['wout'], w['lam16'])
        yc = _from_chunk_rows(y_t, batch, seq)
        yp = _pool(p, seq, w['pool_w'], w['pool_scale'], tile=min(256, seq))
        ym = _attention(q, k, v, seq, tq=min(256, seq))
        x2d = _outmlp(x2d, u, yc, yp, ym, w['ssm_d'], w['glu_w'], w['glu_b'], w['w_out'],
                      w['g_mlp'], w['up'], w['dn'], prm['final_g'], tm=tm,
                      final=(l == depth - 1))
    return x2d.reshape(batch, seq, D_MODEL)


def kernel(x_prompt, x_sample, norm_mix_g, w_in, ssm_a_re, ssm_a_im, ssm_log_dt, ssm_b_re, ssm_b_im, ssm_c_re, ssm_c_im, ssm_d, ssm_glu_w, ssm_glu_b, pool_w, pool_scale, mla_q_norm_g, mla_w_uq, mla_kv_norm_g, mla_w_ukv, w_out, norm_mlp_g, mlp_w_up, mlp_w_down, final_norm_g):
    depth = w_in.shape[0]
    o3 = SSM_WIDTH + POOL_WIDTH + MLA_Q_LORA + MLA_KV_LORA
    win = jnp.concatenate(
        [w_in[..., :o3], _pad_cols(w_in[..., o3:], MLA_NOPE_DIM, HEAD_PAD - MLA_QK_DIM)], axis=-1)
    wuq = _pad_cols(mla_w_uq.reshape(depth, MLA_Q_LORA, MLA_HEADS, MLA_QK_DIM), 0,
                    HEAD_PAD - MLA_QK_DIM).reshape(depth, MLA_Q_LORA, QK_PAD)
    wukv = mla_w_ukv.reshape(depth, MLA_KV_LORA, MLA_HEADS, MLA_NOPE_DIM + MLA_V_DIM)
    wuk = _pad_cols(wukv[..., :MLA_NOPE_DIM], 0, HEAD_PAD - MLA_NOPE_DIM).reshape(
        depth, MLA_KV_LORA, QK_PAD)
    wuv = wukv[..., MLA_NOPE_DIM:].reshape(depth, MLA_KV_LORA, MLA_WIDTH)
    kt, winj, wout, lam16 = jax.vmap(_s5_matrices)(
        ssm_a_re, ssm_a_im, ssm_log_dt, ssm_b_re, ssm_b_im, ssm_c_re, ssm_c_im)
    nwin = len(POOL_WINDOWS)
    eye = jnp.eye(nwin, dtype=F32)
    pool_bd = jnp.einsum('ab,lacd->lacbd', eye, pool_w).reshape(depth, POOL_WIDTH, POOL_WIDTH)
    prm = dict(
        g_mix=norm_mix_g[:, None, :], win=win.astype(BF16),
        qg=mla_q_norm_g[:, None, :], wuq=wuq.astype(BF16),
        kvg=mla_kv_norm_g[:, None, :], wuk=wuk.astype(BF16), wuv=wuv.astype(BF16),
        kt=kt, winj=winj, wout=wout, lam16=lam16,
        pool_w=pool_bd.astype(BF16), pool_scale=pool_scale[:, None, :],
        ssm_d=ssm_d[:, None, :], glu_w=ssm_glu_w.astype(BF16), glu_b=ssm_glu_b[:, None, :],
        w_out=w_out.astype(BF16), g_mlp=norm_mlp_g[:, None, :],
        up=mlp_w_up.astype(BF16), dn=mlp_w_down.astype(BF16),
        final_g=final_norm_g[None, :],
    )
    return (_trunk(x_prompt, prm, depth), _trunk(x_sample, prm, depth))
```

```python
import functools
import math

import jax
import jax.numpy as jnp
from jax import lax
from jax.experimental import pallas as pl
from jax.experimental.pallas import tpu as pltpu

F32 = jnp.float32
BF16 = jnp.bfloat16

D_MODEL = 1024
SSM_WIDTH = 256
SSM_GROUP = 16
SSM_GROUPS = 16
SSM_STATE = 64
POOL_WIDTH = 256
POOL_WINDOWS = (2, 4, 8, 16)
POOL_GROUP = 64
MLA_HEADS = 8
MLA_V_DIM = 64
MLA_NOPE_DIM = 64
MLA_ROPE_DIM = 32
MLA_QK_DIM = MLA_NOPE_DIM + MLA_ROPE_DIM
MLA_Q_LORA = 256
MLA_KV_LORA = 128
MLA_WIDTH = MLA_HEADS * MLA_V_DIM
MLA_SCALE = MLA_QK_DIM ** -0.5
Q_PRESCALE = MLA_SCALE * math.log2(math.e)
ROPE_BASE = 10000.0
D_FF = 4 * D_MODEL
NORM_EPS = 1e-6

LANES = 128
SUBLANES = 8
MXU_DIM = 256
VMEM_BYTES = 64 * 1024 * 1024

HEAD_PAD = LANES
QK_PAD = MLA_HEADS * HEAD_PAD
S5_CHUNK = MXU_DIM // SSM_GROUP
S5_PAIR = 2 * SSM_STATE
S5_PAIRS = SSM_GROUPS // 2
S5_PAIR_COLS = 2 * MXU_DIM
POOL_HALO = max(POOL_WINDOWS)
IN_COLS_PAD = SSM_WIDTH + POOL_WIDTH + MLA_Q_LORA + MLA_KV_LORA + HEAD_PAD
FF_CHUNK = 1024
S5_TOKENS_PER_STEP = 8192
ATTN_SCORE_BYTES = 4 * 1024 * 1024
ATTN_KV_BYTES = 16 * 1024 * 1024


def _vmem_limit(block_bytes):
    return int(min(VMEM_BYTES - 8 * 1024 * 1024, block_bytes))


def _rms(x, g):
    return x * lax.rsqrt(jnp.mean(x * x, axis=-1, keepdims=True) + NORM_EPS) * g


def _const_spec(shape):
    nd = len(shape)
    return pl.BlockSpec(shape, lambda *_: (0,) * nd, pipeline_mode=pl.Buffered(1))


def _inproj_kernel(x_ref, g_ref, win_ref, qg_ref, wuq_ref, kvg_ref, wuk_ref, wuv_ref,
                   cos_ref, sa_ref, sb_ref, u_ref, p_ref, q_ref, k_ref, v_ref):
    h = _rms(x_ref[...], g_ref[...]).astype(BF16)
    proj = jnp.dot(h, win_ref[...], preferred_element_type=F32)
    o0 = SSM_WIDTH
    o1 = o0 + POOL_WIDTH
    o2 = o1 + MLA_Q_LORA
    o3 = o2 + MLA_KV_LORA
    u_ref[...] = proj[:, :o0]
    p_ref[...] = proj[:, o0:o1]
    cos = cos_ref[...]
    sa = sa_ref[...]
    sb = sb_ref[...]

    def rope(t):
        return (t * cos + pltpu.roll(t, HEAD_PAD - MLA_ROPE_DIM // 2, 1) * sa
                + pltpu.roll(t, MLA_ROPE_DIM // 2, 1) * sb)

    qn = _rms(proj[:, o1:o2], qg_ref[...]).astype(BF16)
    q = jnp.dot(qn, wuq_ref[...], preferred_element_type=F32)
    for hd in range(MLA_HEADS):
        sl = slice(hd * HEAD_PAD, (hd + 1) * HEAD_PAD)
        q_ref[:, sl] = (rope(q[:, sl]) * Q_PRESCALE).astype(BF16)
    kn = _rms(proj[:, o2:o3], kvg_ref[...]).astype(BF16)
    kk = jnp.dot(kn, wuk_ref[...], preferred_element_type=F32)
    kr = rope(proj[:, o3:])
    for hd in range(MLA_HEADS):
        sl = slice(hd * HEAD_PAD, (hd + 1) * HEAD_PAD)
        k_ref[:, sl] = (kk[:, sl] + kr).astype(BF16)
    vv = jnp.dot(kn, wuv_ref[...], preferred_element_type=F32)
    lane = lax.broadcasted_iota(jnp.int32, vv.shape, 1)
    v_ref[...] = jnp.where((lane & (HEAD_PAD - 1)) == MLA_V_DIM, 1.0, vv).astype(BF16)


def _inproj(x2d, seq, g, win, qg, wuq, kvg, wuk, wuv, cos_t, sa_t, sb_t, *, tm):
    n = x2d.shape[0]
    nlb = seq // tm
    row = lambda i: (i, 0)
    tab = lambda i: (lax.rem(i, nlb), 0)
    out_shapes = (
        jax.ShapeDtypeStruct((n, SSM_WIDTH), F32),
        jax.ShapeDtypeStruct((n, POOL_WIDTH), F32),
        jax.ShapeDtypeStruct((n, QK_PAD), BF16),
        jax.ShapeDtypeStruct((n, QK_PAD), BF16),
        jax.ShapeDtypeStruct((n, QK_PAD), BF16),
    )
    vmem = (2 * tm * D_MODEL * 4 + 2 * tm * (2 * 256 * 4 + 3 * QK_PAD * 2)
            + 2 * (win.size + wuq.size + wuk.size + wuv.size) + 6 * tm * HEAD_PAD * 4
            + 8 * tm * D_MODEL * 4)
    return pl.pallas_call(
        _inproj_kernel,
        grid=(n // tm,),
        in_specs=[
            pl.BlockSpec((tm, D_MODEL), row),
            _const_spec(g.shape), _const_spec(win.shape), _const_spec(qg.shape),
            _const_spec(wuq.shape), _const_spec(kvg.shape), _const_spec(wuk.shape),
            _const_spec(wuv.shape),
            pl.BlockSpec((tm, HEAD_PAD), tab), pl.BlockSpec((tm, HEAD_PAD), tab),
            pl.BlockSpec((tm, HEAD_PAD), tab),
        ],
        out_specs=(
            pl.BlockSpec((tm, SSM_WIDTH), row), pl.BlockSpec((tm, POOL_WIDTH), row),
            pl.BlockSpec((tm, QK_PAD), row), pl.BlockSpec((tm, QK_PAD), row),
            pl.BlockSpec((tm, QK_PAD), row),
        ),
        out_shape=out_shapes,
        compiler_params=pltpu.CompilerParams(
            dimension_semantics=("parallel",), vmem_limit_bytes=_vmem_limit(vmem)),
        name="inproj",
    )(x2d, g, win, qg, wuq, kvg, wuk, wuv, cos_t, sa_t, sb_t)


GRANULES = LANES // SSM_GROUP
PAIRS_PER_TILE = GRANULES // 2


def _granule_transpose(xs):
    xs = list(xs)
    lane = lax.broadcasted_iota(jnp.int32, xs[0].shape, 1)
    for k in range(3):
        d = 1 << k
        hi = (lane & (SSM_GROUP * d)) != 0
        for r in range(GRANULES):
            if r & d:
                continue
            x, y = xs[r], xs[r + d]
            xs[r] = jnp.where(hi, pltpu.roll(y, SSM_GROUP * d, 1), x)
            xs[r + d] = jnp.where(hi, y, pltpu.roll(x, LANES - SSM_GROUP * d, 1))
    return xs


def _s5_kernel(u_ref, kt_ref, winj_ref, wout_ref, lam_ref, y_ref, uc_ref, inj_ref, st_ref,
               *, nseq, chunks, pitch):
    rows = nseq * chunks

    def to_chunks(m, carry):
        tok = pl.multiple_of(m * (SUBLANES * S5_CHUNK), SUBLANES * S5_CHUNK)
        row = pl.multiple_of(m * SUBLANES, SUBLANES)
        for half in range(2):
            zs = [u_ref[pl.ds(tok + GRANULES * half + r, SUBLANES, stride=S5_CHUNK), :]
                  for r in range(GRANULES)]
            us = _granule_transpose(zs)
            for q in range(GRANULES):
                col = (q % 2) * MXU_DIM + half * LANES
                uc_ref[q // 2, pl.ds(row, SUBLANES), col:col + LANES] = us[q]
        return carry

    lax.fori_loop(0, rows // SUBLANES, to_chunks, 0)

    if nseq < SUBLANES:
        inj_ref[:, nseq * pitch:, :] = jnp.zeros((4, (SUBLANES - nseq) * pitch, LANES), F32)
    zero = jnp.zeros((SUBLANES, LANES), F32)

    for p in range(PAIRS_PER_TILE):
        uv = uc_ref[p].astype(BF16)
        inj = jnp.dot(uv, winj_ref[p], preferred_element_type=F32)
        for b in range(nseq):
            for part in range(4):
                inj_ref[part, b * pitch:b * pitch + chunks, :] = \
                    inj[b * chunks:(b + 1) * chunks, part * LANES:(part + 1) * LANES]
        lam = lam_ref[p]
        lf_re = jnp.broadcast_to(lam[0:1], (SUBLANES, LANES))
        lf_im = jnp.broadcast_to(lam[1:2], (SUBLANES, LANES))
        lb_re = jnp.broadcast_to(lam[2:3], (SUBLANES, LANES))
        lb_im = jnp.broadcast_to(lam[3:4], (SUBLANES, LANES))

        def step(k, carry):
            xf_re, xf_im, xb_re, xb_im = carry
            cf = pl.ds(k, SUBLANES, stride=pitch)
            cb = pl.ds(chunks - 1 - k, SUBLANES, stride=pitch)
            st_ref[0, cf, :] = xf_re
            st_ref[1, cf, :] = xf_im
            st_ref[2, cb, :] = xb_re
            st_ref[3, cb, :] = xb_im
            nf_re = lf_re * xf_re - lf_im * xf_im + inj_ref[0, cf, :]
            nf_im = lf_re * xf_im + lf_im * xf_re + inj_ref[1, cf, :]
            nb_re = lb_re * xb_re - lb_im * xb_im + inj_ref[2, cb, :]
            nb_im = lb_re * xb_im + lb_im * xb_re + inj_ref[3, cb, :]
            return nf_re, nf_im, nb_re, nb_im

        lax.fori_loop(0, chunks, step, (zero, zero, zero, zero))
        st = jnp.concatenate(
            [jnp.concatenate([st_ref[part, b * pitch:b * pitch + chunks, :] for part in range(4)],
                             axis=1) for b in range(nseq)], axis=0)
        y_carry = jnp.dot(st.astype(BF16), wout_ref[p], preferred_element_type=F32)
        y0 = jnp.dot(uv[:, :MXU_DIM], kt_ref[p, 0], preferred_element_type=F32)
        y1 = jnp.dot(uv[:, MXU_DIM:], kt_ref[p, 1], preferred_element_type=F32)
        uc_ref[p, :, :MXU_DIM] = y0 + y_carry[:, :MXU_DIM]
        uc_ref[p, :, MXU_DIM:] = y1 + y_carry[:, MXU_DIM:]

    def to_tokens(m, carry):
        tok = pl.multiple_of(m * (SUBLANES * S5_CHUNK), SUBLANES * S5_CHUNK)
        row = pl.multiple_of(m * SUBLANES, SUBLANES)
        for half in range(2):
            ys = []
            for q in range(GRANULES):
                col = (q % 2) * MXU_DIM + half * LANES
                ys.append(uc_ref[q // 2, pl.ds(row, SUBLANES), col:col + LANES])
            zs = _granule_transpose(ys)
            for r in range(GRANULES):
                y_ref[pl.ds(tok + GRANULES * half + r, SUBLANES, stride=S5_CHUNK), :] = zs[r]
        return carry

    lax.fori_loop(0, rows // SUBLANES, to_tokens, 0)


def _s5(u2d, seq, kt, winj, wout, lam16, *, nseq):
    n = u2d.shape[0]
    chunks = seq // S5_CHUNK
    rows = nseq * chunks
    pitch = chunks + SUBLANES
    blk = nseq * seq
    cols = S5_PAIR_COLS
    vmem = (4 * blk * LANES * 4 + PAIRS_PER_TILE * rows * cols * 4
            + 2 * 4 * SUBLANES * pitch * LANES * 4
            + 2 * 2 * PAIRS_PER_TILE * (2 * MXU_DIM * MXU_DIM + 2 * cols * cols)
            + 8 * rows * cols * 4)
    tile = lambda bg, a: (a, 0, 0)
    return pl.pallas_call(
        functools.partial(_s5_kernel, nseq=nseq, chunks=chunks, pitch=pitch),
        grid=(n // blk, SSM_WIDTH // LANES),
        in_specs=[
            pl.BlockSpec((blk, LANES), lambda bg, a: (bg, a)),
            pl.BlockSpec((PAIRS_PER_TILE, 2, MXU_DIM, MXU_DIM), lambda bg, a: (a, 0, 0, 0)),
            pl.BlockSpec((PAIRS_PER_TILE, cols, cols), tile),
            pl.BlockSpec((PAIRS_PER_TILE, cols, cols), tile),
            pl.BlockSpec((PAIRS_PER_TILE, 4, LANES), tile),
        ],
        out_specs=pl.BlockSpec((blk, LANES), lambda bg, a: (bg, a)),
        out_shape=jax.ShapeDtypeStruct((n, SSM_WIDTH), F32),
        scratch_shapes=[pltpu.VMEM((PAIRS_PER_TILE, rows, cols), F32),
                        pltpu.VMEM((4, SUBLANES * pitch, LANES), F32),
                        pltpu.VMEM((4, SUBLANES * pitch, LANES), F32)],
        compiler_params=pltpu.CompilerParams(
            dimension_semantics=("parallel", "parallel"), vmem_limit_bytes=_vmem_limit(vmem)),
        name="s5_mixer",
    )(u2d, kt, winj, wout, lam16)


def _pool_kernel(p_ref, w_ref, sc_ref, o_ref, pad_ref, *, seq, tile):
    width = POOL_WIDTH
    pad_ref[0:POOL_HALO, :] = jnp.zeros((POOL_HALO, width), F32)
    pad_ref[POOL_HALO + seq:POOL_HALO + seq + POOL_HALO, :] = jnp.zeros((POOL_HALO, width), F32)
    pad_ref[POOL_HALO:POOL_HALO + seq, :] = p_ref[...]
    lane = lax.broadcasted_iota(jnp.int32, (tile, LANES), 1)
    left = lane < POOL_GROUP
    row = lax.broadcasted_iota(jnp.int32, (tile, LANES), 0)
    hw_a = jnp.where(left, POOL_WINDOWS[0] // 2, POOL_WINDOWS[1] // 2)
    hw_b = jnp.where(left, POOL_WINDOWS[2] // 2, POOL_WINDOWS[3] // 2)
    w = w_ref[...]
    sc = sc_ref[...]
    for t0 in range(0, seq, tile):
        halves = []
        for half, hw in ((0, hw_a), (1, hw_b)):
            cols = slice(half * LANES, (half + 1) * LANES)
            wmax = POOL_WINDOWS[2 * half + 1]
            wmin = POOL_WINDOWS[2 * half]
            acc = jnp.zeros((tile, LANES), F32)
            for off in range(-(wmax // 2), wmax // 2):
                xs = pad_ref[POOL_HALO + t0 + off:POOL_HALO + t0 + off + tile, cols]
                if -(wmin // 2) <= off < wmin // 2:
                    acc = acc + xs
                else:
                    acc = acc + jnp.where(left, 0.0, xs)
            pos = row + t0
            lo = jnp.maximum(pos - hw, 0)
            hi = jnp.minimum(pos + hw, seq)
            cnt = (hi - lo).astype(F32)
            centre = pad_ref[POOL_HALO + t0:POOL_HALO + t0 + tile, cols]
            halves.append(acc / cnt - centre)
        pooled = jnp.concatenate(halves, axis=-1).astype(BF16)
        mixed = jnp.dot(pooled, w, preferred_element_type=F32) * sc
        o_ref[t0:t0 + tile, :] = mixed.astype(BF16)


def _pool(p2d, seq, w_bd, scale, *, tile):
    n = p2d.shape[0]
    vmem = 2 * seq * POOL_WIDTH * 4 + 2 * seq * POOL_WIDTH * 2 + (seq + 32) * POOL_WIDTH * 4 \
        + 64 * tile * LANES * 4 + 4 * POOL_WIDTH * POOL_WIDTH
    return pl.pallas_call(
        functools.partial(_pool_kernel, seq=seq, tile=tile),
        grid=(n // seq,),
        in_specs=[pl.BlockSpec((seq, POOL_WIDTH), lambda b: (b, 0)),
                  _const_spec(w_bd.shape), _const_spec(scale.shape)],
        out_specs=pl.BlockSpec((seq, POOL_WIDTH), lambda b: (b, 0)),
        out_shape=jax.ShapeDtypeStruct((n, POOL_WIDTH), BF16),
        scratch_shapes=[pltpu.VMEM((seq + 2 * POOL_HALO, POOL_WIDTH), F32)],
        compiler_params=pltpu.CompilerParams(
            dimension_semantics=("parallel",), vmem_limit_bytes=_vmem_limit(vmem)),
        name="pool_mixer",
    )(p2d, w_bd, scale)


def _attn_kernel(q_ref, k_ref, v_ref, o_ref, *, heads):
    for hh in range(heads):
        q = q_ref[:, hh * HEAD_PAD:(hh + 1) * HEAD_PAD]
        k = k_ref[:, hh * HEAD_PAD:(hh + 1) * HEAD_PAD]
        s = lax.dot_general(q, k, (((1,), (1,)), ((), ())), preferred_element_type=F32)
        m = jnp.max(s, axis=-1, keepdims=True)
        e = jnp.exp2(s - m).astype(BF16)
        o = jnp.dot(e, v_ref[:, hh * HEAD_PAD:(hh + 1) * HEAD_PAD], preferred_element_type=F32)
        o_ref[:, hh * MLA_V_DIM:(hh + 1) * MLA_V_DIM] = (
            o[:, :MLA_V_DIM] / o[:, MLA_V_DIM:MLA_V_DIM + 1]).astype(BF16)


def _attention(q, k, v, seq, *, tq, heads):
    n = q.shape[0]
    nb = n // seq
    nq = seq // tq
    vmem = (2 * tq * heads * HEAD_PAD * 2 + 4 * seq * heads * HEAD_PAD * 2
            + 2 * tq * heads * MLA_V_DIM * 2 + 2 * heads * tq * seq * 4)
    return pl.pallas_call(
        functools.partial(_attn_kernel, heads=heads),
        grid=(nb, MLA_HEADS // heads, nq),
        in_specs=[
            pl.BlockSpec((tq, heads * HEAD_PAD), lambda b, hg, i: (b * nq + i, hg)),
            pl.BlockSpec((seq, heads * HEAD_PAD), lambda b, hg, i: (b, hg)),
            pl.BlockSpec((seq, heads * HEAD_PAD), lambda b, hg, i: (b, hg)),
        ],
        out_specs=pl.BlockSpec((tq, heads * MLA_V_DIM), lambda b, hg, i: (b * nq + i, hg)),
        out_shape=jax.ShapeDtypeStruct((n, MLA_WIDTH), BF16),
        compiler_params=pltpu.CompilerParams(
            dimension_semantics=("parallel", "parallel", "parallel"),
            vmem_limit_bytes=_vmem_limit(vmem)),
        name="mla_attention",
    )(q, k, v)


def _outmlp_kernel(x_ref, u_ref, yc_ref, yp_ref, ym_ref, d_ref, gw_ref, gb_ref, wo_ref,
                   g_ref, up_ref, dn_ref, fg_ref, o_ref, *, final):
    y = yc_ref[...] + d_ref[...] * u_ref[...]
    y = jax.nn.gelu(y)
    gate = jnp.dot(y.astype(BF16), gw_ref[...], preferred_element_type=F32) + gb_ref[...]
    y_ssm = (y * jax.nn.sigmoid(gate)).astype(BF16)
    o1 = SSM_WIDTH + POOL_WIDTH
    mix = jnp.dot(y_ssm, wo_ref[0:SSM_WIDTH, :], preferred_element_type=F32)
    mix = mix + jnp.dot(yp_ref[...], wo_ref[SSM_WIDTH:o1, :], preferred_element_type=F32)
    mix = mix + jnp.dot(ym_ref[...], wo_ref[o1:, :], preferred_element_type=F32)
    x1 = x_ref[...] + mix
    hn = _rms(x1, g_ref[...]).astype(BF16)
    acc = jnp.zeros_like(x1)
    for c in range(0, D_FF, FF_CHUNK):
        hc = jnp.dot(hn, up_ref[:, c:c + FF_CHUNK], preferred_element_type=F32)
        hc = jnp.square(jnp.maximum(hc, 0.0)).astype(BF16)
        acc = acc + jnp.dot(hc, dn_ref[c:c + FF_CHUNK, :], preferred_element_type=F32)
    x2 = x1 + acc
    if final:
        x2 = _rms(x2, fg_ref[...])
    o_ref[...] = x2


def _outmlp(x2d, u, yc, yp, ym, d, gw, gb, wo, g, up, dn, fg, *, tm, final):
    n = x2d.shape[0]
    row = lambda i: (i, 0)
    vmem = (4 * tm * D_MODEL * 4 + 2 * tm * (2 * 256 * 4 + 256 * 2 + MLA_WIDTH * 2)
            + 2 * (gw.size + wo.size + up.size + dn.size)
            + 6 * tm * D_MODEL * 4 + 3 * tm * FF_CHUNK * 4)
    return pl.pallas_call(
        functools.partial(_outmlp_kernel, final=final),
        grid=(n // tm,),
        in_specs=[
            pl.BlockSpec((tm, D_MODEL), row), pl.BlockSpec((tm, SSM_WIDTH), row),
            pl.BlockSpec((tm, SSM_WIDTH), row), pl.BlockSpec((tm, POOL_WIDTH), row),
            pl.BlockSpec((tm, MLA_WIDTH), row),
            _const_spec(d.shape), _const_spec(gw.shape), _const_spec(gb.shape),
            _const_spec(wo.shape), _const_spec(g.shape), _const_spec(up.shape),
            _const_spec(dn.shape), _const_spec(fg.shape),
        ],
        out_specs=pl.BlockSpec((tm, D_MODEL), row),
        out_shape=jax.ShapeDtypeStruct((n, D_MODEL), F32),
        compiler_params=pltpu.CompilerParams(
            dimension_semantics=("parallel",), vmem_limit_bytes=_vmem_limit(vmem)),
        name="outproj_mlp",
    )(x2d, u, yc, yp, ym, d, gw, gb, wo, g, up, dn, fg)


def _rope_tables(seq):
    half = MLA_ROPE_DIM // 2
    inv_freq = ROPE_BASE ** (-jnp.arange(0, MLA_ROPE_DIM, 2, dtype=F32) / MLA_ROPE_DIM)
    ang = jnp.arange(seq, dtype=F32)[:, None] * inv_freq[None, :]
    cos, sin = jnp.cos(ang), jnp.sin(ang)
    z = jnp.zeros_like(cos)
    ones = jnp.ones((seq, MLA_NOPE_DIM), F32)
    pad = jnp.zeros((seq, HEAD_PAD - MLA_QK_DIM), F32)
    zn = jnp.zeros((seq, MLA_NOPE_DIM), F32)
    cos_t = jnp.concatenate([ones, cos, cos, pad], axis=-1)
    sa_t = jnp.concatenate([zn, -sin, z, pad], axis=-1)
    sb_t = jnp.concatenate([zn, z, sin, pad], axis=-1)
    return cos_t, sa_t, sb_t


def _s5_matrices(a_re, a_im, log_dt, b_re, b_im, c_re, c_im):
    hp = lax.Precision.HIGHEST
    t = S5_CHUNK
    g, p, hch = SSM_GROUPS, SSM_STATE, SSM_GROUP
    dt = jnp.exp(log_dt)[..., None]
    mag = jnp.exp(a_re * dt)
    ang = a_im * dt
    lb_re = mag * jnp.cos(ang)
    lb_im = mag * jnp.sin(ang)
    den = a_re * a_re + a_im * a_im
    f_re = ((lb_re - 1.0) * a_re + lb_im * a_im) / den
    f_im = (lb_im * a_re - (lb_re - 1.0) * a_im) / den
    pw_re = [jnp.ones_like(lb_re)]
    pw_im = [jnp.zeros_like(lb_im)]
    for _ in range(t):
        r, i = pw_re[-1], pw_im[-1]
        pw_re.append(r * lb_re - i * lb_im)
        pw_im.append(r * lb_im + i * lb_re)
    pw_re = jnp.stack(pw_re)
    pw_im = jnp.stack(pw_im)
    bb_re = f_re[..., None] * b_re[None] - f_im[..., None] * b_im[None]
    bb_im = f_re[..., None] * b_im[None] + f_im[..., None] * b_re[None]
    w_re = pw_re[..., None] * bb_re[None] - pw_im[..., None] * bb_im[None]
    w_im = pw_re[..., None] * bb_im[None] + pw_im[..., None] * bb_re[None]
    m = (jnp.einsum('gkp,ndgph->ndghk', c_re, w_re, precision=hp)
         - jnp.einsum('gkp,ndgph->ndghk', c_im, w_im, precision=hp))
    jj = jnp.arange(t)[:, None]
    ii = jnp.arange(t)[None, :]
    lag_f = jnp.clip(ii - jj, 0, t - 1)
    lag_b = jnp.clip(jj - ii, 0, t - 1)
    kf = jnp.where((jj <= ii)[:, :, None, None, None], m[lag_f, 0], 0.0)
    kb = jnp.where((jj >= ii)[:, :, None, None, None], m[lag_b, 1], 0.0)
    kt = jnp.transpose(kf + kb, (2, 0, 3, 1, 4)).reshape(g, t * hch, t * hch)
    kt = kt.reshape(S5_PAIRS, 2, MXU_DIM, MXU_DIM)
    rev = jnp.arange(t - 1, -1, -1)
    parts = jnp.stack([w_re[rev, 0], w_im[rev, 0], w_re[:t, 1], w_im[:t, 1]])
    parts = jnp.transpose(parts, (0, 2, 1, 4, 3))
    parts = parts.reshape(4, S5_PAIRS, 2, t, hch, p)
    eye = jnp.eye(2, dtype=F32)
    winj = jnp.einsum('ab,kqajhp->qajhkbp', eye, parts).reshape(S5_PAIRS, S5_PAIR_COLS, 4 * S5_PAIR)
    idx_f = jnp.arange(1, t + 1)
    idx_b = jnp.arange(t, 0, -1)

    def c_times(pr, pi):
        cr = c_re[None] * jnp.transpose(pr, (0, 1, 2))[:, :, None, :] \
            - c_im[None] * pi[:, :, None, :]
        ci = c_re[None] * pi[:, :, None, :] + c_im[None] * pr[:, :, None, :]
        return cr, -ci

    of_re, of_im = c_times(pw_re[idx_f, 0], pw_im[idx_f, 0])
    ob_re, ob_im = c_times(pw_re[idx_b, 1], pw_im[idx_b, 1])
    outs = jnp.stack([of_re, of_im, ob_re, ob_im])
    outs = jnp.transpose(outs, (0, 2, 4, 1, 3)).reshape(4, S5_PAIRS, 2, p, t, hch)
    wout = jnp.einsum('ab,kqapih->kqapbih', eye, outs)
    wout = jnp.transpose(wout, (1, 0, 2, 3, 4, 5, 6)).reshape(S5_PAIRS, 4 * S5_PAIR, S5_PAIR_COLS)
    lam16 = jnp.stack([pw_re[t, 0], pw_im[t, 0], pw_re[t, 1], pw_im[t, 1]])
    lam16 = jnp.transpose(lam16.reshape(4, S5_PAIRS, S5_PAIR), (1, 0, 2))
    return kt.astype(BF16), winj.astype(BF16), wout.astype(BF16), lam16


def _pad_cols(w, lo, hi):
    return jnp.pad(w, [(0, 0)] * (w.ndim - 1) + [(lo, hi)])


def _trunk(x, prm, depth):
    batch, seq, _ = x.shape
    n = batch * seq
    x2d = x.reshape(n, D_MODEL)
    cos_t, sa_t, sb_t = _rope_tables(seq)
    tm = min(512, seq)
    nseq = math.gcd(batch, max(1, min(SUBLANES, S5_TOKENS_PER_STEP // seq)))
    tq = min(seq, ATTN_SCORE_BYTES // (4 * seq))
    heads = max(1, min(MLA_HEADS, ATTN_KV_BYTES // (4 * seq * HEAD_PAD * 2)))
    for l in range(depth):
        w = {k: v[l] for k, v in prm.items() if k != 'final_g'}
        u, p, q, k, v = _inproj(x2d, seq, w['g_mix'], w['win'], w['qg'], w['wuq'], w['kvg'],
                                w['wuk'], w['wuv'], cos_t, sa_t, sb_t, tm=tm)
        yc = _s5(u, seq, w['kt'], w['winj'], w['wout'], w['lam16'], nseq=nseq)
        yp = _pool(p, seq, w['pool_w'], w['pool_scale'], tile=min(256, seq))
        ym = _attention(q, k, v, seq, tq=tq, heads=heads)
        x2d = _outmlp(x2d, u, yc, yp, ym, w['ssm_d'], w['glu_w'], w['glu_b'], w['w_out'],
                      w['g_mlp'], w['up'], w['dn'], prm['final_g'], tm=tm,
                      final=(l == depth - 1))
    return x2d.reshape(batch, seq, D_MODEL)


def kernel(x_prompt, x_sample, norm_mix_g, w_in, ssm_a_re, ssm_a_im, ssm_log_dt, ssm_b_re, ssm_b_im, ssm_c_re, ssm_c_im, ssm_d, ssm_glu_w, ssm_glu_b, pool_w, pool_scale, mla_q_norm_g, mla_w_uq, mla_kv_norm_g, mla_w_ukv, w_out, norm_mlp_g, mlp_w_up, mlp_w_down, final_norm_g):
    depth = w_in.shape[0]
    o3 = SSM_WIDTH + POOL_WIDTH + MLA_Q_LORA + MLA_KV_LORA
    win = jnp.concatenate(
        [w_in[..., :o3], _pad_cols(w_in[..., o3:], MLA_NOPE_DIM, HEAD_PAD - MLA_QK_DIM)], axis=-1)
    wuq = _pad_cols(mla_w_uq.reshape(depth, MLA_Q_LORA, MLA_HEADS, MLA_QK_DIM), 0,
                    HEAD_PAD - MLA_QK_DIM).reshape(depth, MLA_Q_LORA, QK_PAD)
    wukv = mla_w_ukv.reshape(depth, MLA_KV_LORA, MLA_HEADS, MLA_NOPE_DIM + MLA_V_DIM)
    wuk = _pad_cols(wukv[..., :MLA_NOPE_DIM], 0, HEAD_PAD - MLA_NOPE_DIM).reshape(
        depth, MLA_KV_LORA, QK_PAD)
    wuv = _pad_cols(wukv[..., MLA_NOPE_DIM:], 0, HEAD_PAD - MLA_V_DIM).reshape(
        depth, MLA_KV_LORA, QK_PAD)
    kt, winj, wout, lam16 = jax.vmap(_s5_matrices)(
        ssm_a_re, ssm_a_im, ssm_log_dt, ssm_b_re, ssm_b_im, ssm_c_re, ssm_c_im)
    nwin = len(POOL_WINDOWS)
    eye = jnp.eye(nwin, dtype=F32)
    pool_bd = jnp.einsum('ab,lacd->lacbd', eye, pool_w).reshape(depth, POOL_WIDTH, POOL_WIDTH)
    prm = dict(
        g_mix=norm_mix_g[:, None, :], win=win.astype(BF16),
        qg=mla_q_norm_g[:, None, :], wuq=wuq.astype(BF16),
        kvg=mla_kv_norm_g[:, None, :], wuk=wuk.astype(BF16), wuv=wuv.astype(BF16),
        kt=kt, winj=winj, wout=wout, lam16=lam16,
        pool_w=pool_bd.astype(BF16), pool_scale=pool_scale[:, None, :],
        ssm_d=ssm_d[:, None, :], glu_w=ssm_glu_w.astype(BF16), glu_b=ssm_glu_b[:, None, :],
        w_out=w_out.astype(BF16), g_mlp=norm_mlp_g[:, None, :],
        up=mlp_w_up.astype(BF16), dn=mlp_w_down.astype(BF16),
        final_g=final_norm_g[None, :],
    )
    return (_trunk(x_prompt, prm, depth), _trunk(x_sample, prm, depth))
```

```python
import functools
import math

import jax
import jax.numpy as jnp
from jax import lax
from jax.experimental import pallas as pl
from jax.experimental.pallas import tpu as pltpu

F32 = jnp.float32
BF16 = jnp.bfloat16

D_MODEL = 1024
SSM_WIDTH = 256
SSM_GROUP = 16
SSM_GROUPS = 16
SSM_STATE = 64
POOL_WIDTH = 256
POOL_WINDOWS = (2, 4, 8, 16)
POOL_GROUP = 64
MLA_HEADS = 8
MLA_V_DIM = 64
MLA_NOPE_DIM = 64
MLA_ROPE_DIM = 32
MLA_QK_DIM = MLA_NOPE_DIM + MLA_ROPE_DIM
MLA_Q_LORA = 256
MLA_KV_LORA = 128
MLA_WIDTH = MLA_HEADS * MLA_V_DIM
MLA_SCALE = MLA_QK_DIM ** -0.5
Q_PRESCALE = MLA_SCALE * math.log2(math.e)
ROPE_BASE = 10000.0
D_FF = 4 * D_MODEL
NORM_EPS = 1e-6

LANES = 128
SUBLANES = 8
MXU_DIM = 256
VMEM_BYTES = 64 * 1024 * 1024

HEAD_PAD = LANES
QK_PAD = MLA_HEADS * HEAD_PAD
S5_CHUNK = MXU_DIM // SSM_GROUP
S5_PAIR = 2 * SSM_STATE
S5_PAIRS = SSM_GROUPS // 2
S5_PAIR_COLS = 2 * MXU_DIM
POOL_HALO = max(POOL_WINDOWS)
IN_COLS_PAD = SSM_WIDTH + POOL_WIDTH + MLA_Q_LORA + MLA_KV_LORA + HEAD_PAD
FF_CHUNK = 1024
INPROJ_ROWS = 1024
INPROJ_SUB_ROWS = 256
S5_TOKENS_PER_STEP = 8192
ATTN_SCORE_BYTES = 4 * 1024 * 1024
ATTN_KV_BYTES = 16 * 1024 * 1024


def _vmem_limit(block_bytes):
    return int(min(VMEM_BYTES - 8 * 1024 * 1024, block_bytes))


def _rms(x, g):
    return x * lax.rsqrt(jnp.mean(x * x, axis=-1, keepdims=True) + NORM_EPS) * g


def _const_spec(shape):
    nd = len(shape)
    return pl.BlockSpec(shape, lambda *_: (0,) * nd, pipeline_mode=pl.Buffered(1))


def _inproj_kernel(x_ref, g_ref, win_ref, qg_ref, wuq_ref, kvg_ref, wuk_ref, wuv_ref,
                   cos_ref, sa_ref, sb_ref, u_ref, p_ref, q_ref, k_ref, v_ref, *, sub):
    o0 = SSM_WIDTH
    o1 = o0 + POOL_WIDTH
    o2 = o1 + MLA_Q_LORA
    o3 = o2 + MLA_KV_LORA
    nsub = x_ref.shape[0] // sub

    def front(i):
        rs = slice(i * sub, (i + 1) * sub)
        h = _rms(x_ref[rs, :], g_ref[...]).astype(BF16)
        return jnp.dot(h, win_ref[...], preferred_element_type=F32)

    def back(i, proj):
        rs = slice(i * sub, (i + 1) * sub)
        u_ref[rs, :] = proj[:, :o0]
        p_ref[rs, :] = proj[:, o0:o1]
        cos = cos_ref[rs, :]
        sa = sa_ref[rs, :]
        sb = sb_ref[rs, :]

        def rope(t):
            return (t * cos + pltpu.roll(t, HEAD_PAD - MLA_ROPE_DIM // 2, 1) * sa
                    + pltpu.roll(t, MLA_ROPE_DIM // 2, 1) * sb)

        qn = _rms(proj[:, o1:o2], qg_ref[...]).astype(BF16)
        q = jnp.dot(qn, wuq_ref[...], preferred_element_type=F32)
        for hd in range(MLA_HEADS):
            sl = slice(hd * HEAD_PAD, (hd + 1) * HEAD_PAD)
            q_ref[rs, sl] = (rope(q[:, sl]) * Q_PRESCALE).astype(BF16)
        kn = _rms(proj[:, o2:o3], kvg_ref[...]).astype(BF16)
        kk = jnp.dot(kn, wuk_ref[...], preferred_element_type=F32)
        kr = rope(proj[:, o3:])
        for hd in range(MLA_HEADS):
            sl = slice(hd * HEAD_PAD, (hd + 1) * HEAD_PAD)
            k_ref[rs, sl] = (kk[:, sl] + kr).astype(BF16)
        vv = jnp.dot(kn, wuv_ref[...], preferred_element_type=F32)
        lane = lax.broadcasted_iota(jnp.int32, vv.shape, 1)
        v_ref[rs, :] = jnp.where((lane & (HEAD_PAD - 1)) == MLA_V_DIM, 1.0, vv).astype(BF16)

    prev = None
    for i in range(nsub + 1):
        cur = front(i) if i < nsub else None
        if prev is not None:
            back(i - 1, prev)
        prev = cur


def _inproj(x2d, seq, g, win, qg, wuq, kvg, wuk, wuv, cos_t, sa_t, sb_t, *, tm, sub):
    n = x2d.shape[0]
    nlb = seq // tm
    row = lambda i: (i, 0)
    tab = lambda i: (lax.rem(i, nlb), 0)
    out_shapes = (
        jax.ShapeDtypeStruct((n, SSM_WIDTH), F32),
        jax.ShapeDtypeStruct((n, POOL_WIDTH), F32),
        jax.ShapeDtypeStruct((n, QK_PAD), BF16),
        jax.ShapeDtypeStruct((n, QK_PAD), BF16),
        jax.ShapeDtypeStruct((n, QK_PAD), BF16),
    )
    vmem = (2 * tm * D_MODEL * 4 + 2 * tm * (2 * 256 * 4 + 3 * QK_PAD * 2)
            + 2 * (win.size + wuq.size + wuk.size + wuv.size) + 6 * tm * HEAD_PAD * 4
            + 8 * tm * D_MODEL * 4)
    return pl.pallas_call(
        functools.partial(_inproj_kernel, sub=sub),
        grid=(n // tm,),
        in_specs=[
            pl.BlockSpec((tm, D_MODEL), row),
            _const_spec(g.shape), _const_spec(win.shape), _const_spec(qg.shape),
            _const_spec(wuq.shape), _const_spec(kvg.shape), _const_spec(wuk.shape),
            _const_spec(wuv.shape),
            pl.BlockSpec((tm, HEAD_PAD), tab), pl.BlockSpec((tm, HEAD_PAD), tab),
            pl.BlockSpec((tm, HEAD_PAD), tab),
        ],
        out_specs=(
            pl.BlockSpec((tm, SSM_WIDTH), row), pl.BlockSpec((tm, POOL_WIDTH), row),
            pl.BlockSpec((tm, QK_PAD), row), pl.BlockSpec((tm, QK_PAD), row),
            pl.BlockSpec((tm, QK_PAD), row),
        ),
        out_shape=out_shapes,
        compiler_params=pltpu.CompilerParams(
            dimension_semantics=("parallel",), vmem_limit_bytes=_vmem_limit(vmem)),
        name="inproj",
    )(x2d, g, win, qg, wuq, kvg, wuk, wuv, cos_t, sa_t, sb_t)


GRANULES = LANES // SSM_GROUP
PAIRS_PER_TILE = GRANULES // 2


def _granule_swap_matrix():
    idx = jnp.arange(GRANULES * LANES)
    r, q, h = idx // LANES, (idx // SSM_GROUP) % GRANULES, idx % SSM_GROUP
    return (idx[None, :] == (q * LANES + r * SSM_GROUP + h)[:, None]).astype(BF16)


def _s5_kernel(u_ref, perm_ref, kt_ref, winj_ref, wout_ref, lam_ref, y_ref, uc_ref, yc_ref,
               inj_ref, st_ref, *, nseq, chunks, pitch):
    rows = nseq * chunks
    perm = perm_ref[...]

    for half in range(2):
        zs = jnp.concatenate(
            [u_ref[pl.ds(GRANULES * half + r, rows, stride=S5_CHUNK), :].astype(BF16)
             for r in range(GRANULES)], axis=1)
        us = jnp.dot(zs, perm, preferred_element_type=F32).astype(BF16)
        for q in range(GRANULES):
            col = (q % 2) * MXU_DIM + half * LANES
            uc_ref[q // 2, :, col:col + LANES] = us[:, q * LANES:(q + 1) * LANES]

    if nseq < SUBLANES:
        inj_ref[:, nseq * pitch:, :] = jnp.zeros((4, (SUBLANES - nseq) * pitch, LANES), F32)
    zero = jnp.zeros((SUBLANES, LANES), F32)

    for p in range(PAIRS_PER_TILE):
        uv = uc_ref[p]
        inj = jnp.dot(uv, winj_ref[p], preferred_element_type=F32)
        for b in range(nseq):
            for part in range(4):
                inj_ref[part, b * pitch:b * pitch + chunks, :] = \
                    inj[b * chunks:(b + 1) * chunks, part * LANES:(part + 1) * LANES]
        lam = lam_ref[p]
        lf_re = jnp.broadcast_to(lam[0:1], (SUBLANES, LANES))
        lf_im = jnp.broadcast_to(lam[1:2], (SUBLANES, LANES))
        lb_re = jnp.broadcast_to(lam[2:3], (SUBLANES, LANES))
        lb_im = jnp.broadcast_to(lam[3:4], (SUBLANES, LANES))

        def step(k, carry):
            xf_re, xf_im, xb_re, xb_im = carry
            cf = pl.ds(k, SUBLANES, stride=pitch)
            cb = pl.ds(chunks - 1 - k, SUBLANES, stride=pitch)
            st_ref[0, cf, :] = xf_re
            st_ref[1, cf, :] = xf_im
            st_ref[2, cb, :] = xb_re
            st_ref[3, cb, :] = xb_im
            nf_re = lf_re * xf_re - lf_im * xf_im + inj_ref[0, cf, :]
            nf_im = lf_re * xf_im + lf_im * xf_re + inj_ref[1, cf, :]
            nb_re = lb_re * xb_re - lb_im * xb_im + inj_ref[2, cb, :]
            nb_im = lb_re * xb_im + lb_im * xb_re + inj_ref[3, cb, :]
            return nf_re, nf_im, nb_re, nb_im

        lax.fori_loop(0, chunks, step, (zero, zero, zero, zero))
        st = jnp.concatenate(
            [jnp.concatenate([st_ref[part, b * pitch:b * pitch + chunks, :] for part in range(4)],
                             axis=1) for b in range(nseq)], axis=0)
        y_carry = jnp.dot(st.astype(BF16), wout_ref[p], preferred_element_type=F32)
        y0 = jnp.dot(uv[:, :MXU_DIM], kt_ref[p, 0], preferred_element_type=F32)
        y1 = jnp.dot(uv[:, MXU_DIM:], kt_ref[p, 1], preferred_element_type=F32)
        yc_ref[p, :, :MXU_DIM] = y0 + y_carry[:, :MXU_DIM]
        yc_ref[p, :, MXU_DIM:] = y1 + y_carry[:, MXU_DIM:]

    for half in range(2):
        ys = jnp.concatenate(
            [yc_ref[q // 2, :, (q % 2) * MXU_DIM + half * LANES:(q % 2) * MXU_DIM + (half + 1) * LANES]
             for q in range(GRANULES)], axis=1)
        hi = ys.astype(BF16)
        rest = ys - hi.astype(F32)
        mid = rest.astype(BF16)
        lo = (rest - mid.astype(F32)).astype(BF16)
        zs = (jnp.dot(hi, perm, preferred_element_type=F32)
              + jnp.dot(mid, perm, preferred_element_type=F32)
              + jnp.dot(lo, perm, preferred_element_type=F32))
        for r in range(GRANULES):
            y_ref[pl.ds(GRANULES * half + r, rows, stride=S5_CHUNK), :] = zs[:, r * LANES:(r + 1) * LANES]


def _s5(u2d, seq, perm, kt, winj, wout, lam16, *, nseq):
    n = u2d.shape[0]
    chunks = seq // S5_CHUNK
    rows = nseq * chunks
    pitch = chunks + SUBLANES
    blk = nseq * seq
    cols = S5_PAIR_COLS
    vmem = (4 * blk * LANES * 4 + PAIRS_PER_TILE * rows * cols * (2 + 4)
            + 2 * 4 * SUBLANES * pitch * LANES * 4 + 2 * perm.size
            + 2 * 2 * PAIRS_PER_TILE * (2 * MXU_DIM * MXU_DIM + 2 * cols * cols)
            + 8 * rows * GRANULES * LANES * 4)
    tile = lambda bg, a: (a, 0, 0)
    return pl.pallas_call(
        functools.partial(_s5_kernel, nseq=nseq, chunks=chunks, pitch=pitch),
        grid=(n // blk, SSM_WIDTH // LANES),
        in_specs=[
            pl.BlockSpec((blk, LANES), lambda bg, a: (bg, a)),
            _const_spec(perm.shape),
            pl.BlockSpec((PAIRS_PER_TILE, 2, MXU_DIM, MXU_DIM), lambda bg, a: (a, 0, 0, 0)),
            pl.BlockSpec((PAIRS_PER_TILE, cols, cols), tile),
            pl.BlockSpec((PAIRS_PER_TILE, cols, cols), tile),
            pl.BlockSpec((PAIRS_PER_TILE, 4, LANES), tile),
        ],
        out_specs=pl.BlockSpec((blk, LANES), lambda bg, a: (bg, a)),
        out_shape=jax.ShapeDtypeStruct((n, SSM_WIDTH), F32),
        scratch_shapes=[pltpu.VMEM((PAIRS_PER_TILE, rows, cols), BF16),
                        pltpu.VMEM((PAIRS_PER_TILE, rows, cols), F32),
                        pltpu.VMEM((4, SUBLANES * pitch, LANES), F32),
                        pltpu.VMEM((4, SUBLANES * pitch, LANES), F32)],
        compiler_params=pltpu.CompilerParams(
            dimension_semantics=("parallel", "parallel"), vmem_limit_bytes=_vmem_limit(vmem)),
        name="s5_mixer",
    )(u2d, perm, kt, winj, wout, lam16)


def _pool_kernel(p_ref, w_ref, sc_ref, o_ref, pad_ref, *, seq, tile):
    width = POOL_WIDTH
    pad_ref[0:POOL_HALO, :] = jnp.zeros((POOL_HALO, width), F32)
    pad_ref[POOL_HALO + seq:POOL_HALO + seq + POOL_HALO, :] = jnp.zeros((POOL_HALO, width), F32)
    pad_ref[POOL_HALO:POOL_HALO + seq, :] = p_ref[...]
    lane = lax.broadcasted_iota(jnp.int32, (tile, LANES), 1)
    left = lane < POOL_GROUP
    row = lax.broadcasted_iota(jnp.int32, (tile, LANES), 0)
    hw_a = jnp.where(left, POOL_WINDOWS[0] // 2, POOL_WINDOWS[1] // 2)
    hw_b = jnp.where(left, POOL_WINDOWS[2] // 2, POOL_WINDOWS[3] // 2)
    w = w_ref[...]
    sc = sc_ref[...]
    for t0 in range(0, seq, tile):
        halves = []
        for half, hw in ((0, hw_a), (1, hw_b)):
            cols = slice(half * LANES, (half + 1) * LANES)
            wmax = POOL_WINDOWS[2 * half + 1]
            wmin = POOL_WINDOWS[2 * half]
            acc = jnp.zeros((tile, LANES), F32)
            for off in range(-(wmax // 2), wmax // 2):
                xs = pad_ref[POOL_HALO + t0 + off:POOL_HALO + t0 + off + tile, cols]
                if -(wmin // 2) <= off < wmin // 2:
                    acc = acc + xs
                else:
                    acc = acc + jnp.where(left, 0.0, xs)
            pos = row + t0
            lo = jnp.maximum(pos - hw, 0)
            hi = jnp.minimum(pos + hw, seq)
            cnt = (hi - lo).astype(F32)
            centre = pad_ref[POOL_HALO + t0:POOL_HALO + t0 + tile, cols]
            halves.append(acc / cnt - centre)
        pooled = jnp.concatenate(halves, axis=-1).astype(BF16)
        mixed = jnp.dot(pooled, w, preferred_element_type=F32) * sc
        o_ref[t0:t0 + tile, :] = mixed.astype(BF16)


def _pool(p2d, seq, w_bd, scale, *, tile):
    n = p2d.shape[0]
    vmem = 2 * seq * POOL_WIDTH * 4 + 2 * seq * POOL_WIDTH * 2 + (seq + 32) * POOL_WIDTH * 4 \
        + 64 * tile * LANES * 4 + 4 * POOL_WIDTH * POOL_WIDTH
    return pl.pallas_call(
        functools.partial(_pool_kernel, seq=seq, tile=tile),
        grid=(n // seq,),
        in_specs=[pl.BlockSpec((seq, POOL_WIDTH), lambda b: (b, 0)),
                  _const_spec(w_bd.shape), _const_spec(scale.shape)],
        out_specs=pl.BlockSpec((seq, POOL_WIDTH), lambda b: (b, 0)),
        out_shape=jax.ShapeDtypeStruct((n, POOL_WIDTH), BF16),
        scratch_shapes=[pltpu.VMEM((seq + 2 * POOL_HALO, POOL_WIDTH), F32)],
        compiler_params=pltpu.CompilerParams(
            dimension_semantics=("parallel",), vmem_limit_bytes=_vmem_limit(vmem)),
        name="pool_mixer",
    )(p2d, w_bd, scale)


def _attn_kernel(q_ref, k_ref, v_ref, o_ref, *, heads):
    def scores(hh):
        q = q_ref[:, hh * HEAD_PAD:(hh + 1) * HEAD_PAD]
        k = k_ref[:, hh * HEAD_PAD:(hh + 1) * HEAD_PAD]
        return lax.dot_general(q, k, (((1,), (1,)), ((), ())), preferred_element_type=F32)

    s_next = scores(0)
    for hh in range(heads):
        s = s_next
        if hh + 1 < heads:
            s_next = scores(hh + 1)
        m = jnp.max(s, axis=-1, keepdims=True)
        e = jnp.exp2(s - m).astype(BF16)
        o = jnp.dot(e, v_ref[:, hh * HEAD_PAD:(hh + 1) * HEAD_PAD], preferred_element_type=F32)
        o_ref[:, hh * MLA_V_DIM:(hh + 1) * MLA_V_DIM] = (
            o[:, :MLA_V_DIM] / o[:, MLA_V_DIM:MLA_V_DIM + 1]).astype(BF16)


def _attention(q, k, v, seq, *, tq, heads):
    n = q.shape[0]
    nb = n // seq
    nq = seq // tq
    vmem = (2 * tq * heads * HEAD_PAD * 2 + 4 * seq * heads * HEAD_PAD * 2
            + 2 * tq * heads * MLA_V_DIM * 2 + 2 * heads * tq * seq * 4)
    return pl.pallas_call(
        functools.partial(_attn_kernel, heads=heads),
        grid=(nb, MLA_HEADS // heads, nq),
        in_specs=[
            pl.BlockSpec((tq, heads * HEAD_PAD), lambda b, hg, i: (b * nq + i, hg)),
            pl.BlockSpec((seq, heads * HEAD_PAD), lambda b, hg, i: (b, hg)),
            pl.BlockSpec((seq, heads * HEAD_PAD), lambda b, hg, i: (b, hg)),
        ],
        out_specs=pl.BlockSpec((tq, heads * MLA_V_DIM), lambda b, hg, i: (b * nq + i, hg)),
        out_shape=jax.ShapeDtypeStruct((n, MLA_WIDTH), BF16),
        compiler_params=pltpu.CompilerParams(
            dimension_semantics=("parallel", "parallel", "parallel"),
            vmem_limit_bytes=_vmem_limit(vmem)),
        name="mla_attention",
    )(q, k, v)


def _outmlp_kernel(x_ref, u_ref, yc_ref, yp_ref, ym_ref, d_ref, gw_ref, gb_ref, wo_ref,
                   g_ref, up_ref, dn_ref, fg_ref, o_ref, *, final):
    y = yc_ref[...] + d_ref[...] * u_ref[...]
    y = jax.nn.gelu(y)
    gate = jnp.dot(y.astype(BF16), gw_ref[...], preferred_element_type=F32) + gb_ref[...]
    y_ssm = (y * jax.nn.sigmoid(gate)).astype(BF16)
    o1 = SSM_WIDTH + POOL_WIDTH
    mix = jnp.dot(y_ssm, wo_ref[0:SSM_WIDTH, :], preferred_element_type=F32)
    mix = mix + jnp.dot(yp_ref[...], wo_ref[SSM_WIDTH:o1, :], preferred_element_type=F32)
    mix = mix + jnp.dot(ym_ref[...], wo_ref[o1:, :], preferred_element_type=F32)
    x1 = x_ref[...] + mix
    hn = _rms(x1, g_ref[...]).astype(BF16)
    acc = jnp.zeros_like(x1)
    for c in range(0, D_FF, FF_CHUNK):
        hc = jnp.dot(hn, up_ref[:, c:c + FF_CHUNK], preferred_element_type=F32)
        hc = jnp.square(jnp.maximum(hc, 0.0)).astype(BF16)
        acc = acc + jnp.dot(hc, dn_ref[c:c + FF_CHUNK, :], preferred_element_type=F32)
    x2 = x1 + acc
    if final:
        x2 = _rms(x2, fg_ref[...])
    o_ref[...] = x2


def _outmlp(x2d, u, yc, yp, ym, d, gw, gb, wo, g, up, dn, fg, *, tm, final):
    n = x2d.shape[0]
    row = lambda i: (i, 0)
    vmem = (4 * tm * D_MODEL * 4 + 2 * tm * (2 * 256 * 4 + 256 * 2 + MLA_WIDTH * 2)
            + 2 * (gw.size + wo.size + up.size + dn.size)
            + 6 * tm * D_MODEL * 4 + 3 * tm * FF_CHUNK * 4)
    return pl.pallas_call(
        functools.partial(_outmlp_kernel, final=final),
        grid=(n // tm,),
        in_specs=[
            pl.BlockSpec((tm, D_MODEL), row), pl.BlockSpec((tm, SSM_WIDTH), row),
            pl.BlockSpec((tm, SSM_WIDTH), row), pl.BlockSpec((tm, POOL_WIDTH), row),
            pl.BlockSpec((tm, MLA_WIDTH), row),
            _const_spec(d.shape), _const_spec(gw.shape), _const_spec(gb.shape),
            _const_spec(wo.shape), _const_spec(g.shape), _const_spec(up.shape),
            _const_spec(dn.shape), _const_spec(fg.shape),
        ],
        out_specs=pl.BlockSpec((tm, D_MODEL), row),
        out_shape=jax.ShapeDtypeStruct((n, D_MODEL), F32),
        compiler_params=pltpu.CompilerParams(
            dimension_semantics=("parallel",), vmem_limit_bytes=_vmem_limit(vmem)),
        name="outproj_mlp",
    )(x2d, u, yc, yp, ym, d, gw, gb, wo, g, up, dn, fg)


def _rope_tables(seq):
    half = MLA_ROPE_DIM // 2
    inv_freq = ROPE_BASE ** (-jnp.arange(0, MLA_ROPE_DIM, 2, dtype=F32) / MLA_ROPE_DIM)
    ang = jnp.arange(seq, dtype=F32)[:, None] * inv_freq[None, :]
    cos, sin = jnp.cos(ang), jnp.sin(ang)
    z = jnp.zeros_like(cos)
    ones = jnp.ones((seq, MLA_NOPE_DIM), F32)
    pad = jnp.zeros((seq, HEAD_PAD - MLA_QK_DIM), F32)
    zn = jnp.zeros((seq, MLA_NOPE_DIM), F32)
    cos_t = jnp.concatenate([ones, cos, cos, pad], axis=-1)
    sa_t = jnp.concatenate([zn, -sin, z, pad], axis=-1)
    sb_t = jnp.concatenate([zn, z, sin, pad], axis=-1)
    return cos_t, sa_t, sb_t


def _s5_matrices(a_re, a_im, log_dt, b_re, b_im, c_re, c_im):
    hp = lax.Precision.HIGHEST
    t = S5_CHUNK
    g, p, hch = SSM_GROUPS, SSM_STATE, SSM_GROUP
    dt = jnp.exp(log_dt)[..., None]
    mag = jnp.exp(a_re * dt)
    ang = a_im * dt
    lb_re = mag * jnp.cos(ang)
    lb_im = mag * jnp.sin(ang)
    den = a_re * a_re + a_im * a_im
    f_re = ((lb_re - 1.0) * a_re + lb_im * a_im) / den
    f_im = (lb_im * a_re - (lb_re - 1.0) * a_im) / den
    pw_re = [jnp.ones_like(lb_re)]
    pw_im = [jnp.zeros_like(lb_im)]
    for _ in range(t):
        r, i = pw_re[-1], pw_im[-1]
        pw_re.append(r * lb_re - i * lb_im)
        pw_im.append(r * lb_im + i * lb_re)
    pw_re = jnp.stack(pw_re)
    pw_im = jnp.stack(pw_im)
    bt_re = jnp.swapaxes(b_re, 1, 2)
    bt_im = jnp.swapaxes(b_im, 1, 2)
    fr, fi = f_re[:, :, None, :], f_im[:, :, None, :]
    bb_re = fr * bt_re[None] - fi * bt_im[None]
    bb_im = fr * bt_im[None] + fi * bt_re[None]
    pr, pi = pw_re[:, :, :, None, :], pw_im[:, :, :, None, :]
    w_re = pr * bb_re[None] - pi * bb_im[None]
    w_im = pr * bb_im[None] + pi * bb_re[None]

    def strip(w):
        return jnp.concatenate([w[t - 1:0:-1, 1], (w[0, 0] + w[0, 1])[None], w[1:t, 0]], axis=0)

    lagk = (jnp.einsum('gkp,lghp->ghlk', c_re, strip(w_re), precision=hp)
            - jnp.einsum('gkp,lghp->ghlk', c_im, strip(w_im), precision=hp))
    lagk = lagk.reshape(g, hch, (2 * t - 1) * hch)
    kt = jnp.stack([lagk[:, :, (t - 1 - j) * hch:(t - 1 - j) * hch + t * hch] for j in range(t)],
                   axis=1)
    kt = kt.reshape(S5_PAIRS, 2, MXU_DIM, MXU_DIM)

    def pair_blocks(parts):
        x = jnp.stack(parts, axis=3).reshape(S5_PAIRS, 2, t, hch, 4, p)
        even = jnp.pad(x[:, 0], [(0, 0)] * 4 + [(0, p)])
        odd = jnp.pad(x[:, 1], [(0, 0)] * 4 + [(p, 0)])
        return jnp.stack([even, odd], axis=1).reshape(S5_PAIRS, S5_PAIR_COLS, 4 * S5_PAIR)

    def by_group(w):
        return jnp.swapaxes(w, 0, 1)

    winj = pair_blocks([by_group(w_re[t - 1::-1, 0][:t]), by_group(w_im[t - 1::-1, 0][:t]),
                        by_group(w_re[:t, 1]), by_group(w_im[:t, 1])])

    def c_times(qr, qi):
        qr, qi = jnp.swapaxes(qr, 0, 1)[:, :, None, :], jnp.swapaxes(qi, 0, 1)[:, :, None, :]
        cr, ci = c_re[:, None], c_im[:, None]
        return cr * qr - ci * qi, -(cr * qi + ci * qr)

    of_re, of_im = c_times(pw_re[1:t + 1, 0], pw_im[1:t + 1, 0])
    ob_re, ob_im = c_times(pw_re[t:0:-1, 1], pw_im[t:0:-1, 1])
    wout = jnp.swapaxes(pair_blocks([of_re, of_im, ob_re, ob_im]), 1, 2)
    lam16 = jnp.stack([pw_re[t, 0], pw_im[t, 0], pw_re[t, 1], pw_im[t, 1]])
    lam16 = jnp.transpose(lam16.reshape(4, S5_PAIRS, S5_PAIR), (1, 0, 2))
    return kt.astype(BF16), winj.astype(BF16), wout.astype(BF16), lam16


def _pad_cols(w, lo, hi):
    return jnp.pad(w, [(0, 0)] * (w.ndim - 1) + [(lo, hi)])


def _trunk(x, prm, depth):
    batch, seq, _ = x.shape
    n = batch * seq
    x2d = x.reshape(n, D_MODEL)
    cos_t, sa_t, sb_t = _rope_tables(seq)
    tm = min(512, seq)
    nseq = math.gcd(batch, max(1, min(SUBLANES, S5_TOKENS_PER_STEP // seq)))
    tq = min(seq, ATTN_SCORE_BYTES // (4 * seq))
    heads = max(1, min(MLA_HEADS, ATTN_KV_BYTES // (4 * seq * HEAD_PAD * 2)))
    perm = _granule_swap_matrix()
    for l in range(depth):
        w = {k: v[l] for k, v in prm.items() if k != 'final_g'}
        u, p, q, k, v = _inproj(x2d, seq, w['g_mix'], w['win'], w['qg'], w['wuq'], w['kvg'],
                                w['wuk'], w['wuv'], cos_t, sa_t, sb_t,
                                tm=min(INPROJ_ROWS, seq), sub=min(INPROJ_SUB_ROWS, seq))
        yc = _s5(u, seq, perm, w['kt'], w['winj'], w['wout'], w['lam16'], nseq=nseq)
        yp = _pool(p, seq, w['pool_w'], w['pool_scale'], tile=min(256, seq))
        ym = _attention(q, k, v, seq, tq=tq, heads=heads)
        x2d = _outmlp(x2d, u, yc, yp, ym, w['ssm_d'], w['glu_w'], w['glu_b'], w['w_out'],
                      w['g_mlp'], w['up'], w['dn'], prm['final_g'], tm=tm,
                      final=(l == depth - 1))
    return x2d.reshape(batch, seq, D_MODEL)


def kernel(x_prompt, x_sample, norm_mix_g, w_in, ssm_a_re, ssm_a_im, ssm_log_dt, ssm_b_re, ssm_b_im, ssm_c_re, ssm_c_im, ssm_d, ssm_glu_w, ssm_glu_b, pool_w, pool_scale, mla_q_norm_g, mla_w_uq, mla_kv_norm_g, mla_w_ukv, w_out, norm_mlp_g, mlp_w_up, mlp_w_down, final_norm_g):
    depth = w_in.shape[0]
    o3 = SSM_WIDTH + POOL_WIDTH + MLA_Q_LORA + MLA_KV_LORA
    win = jnp.concatenate(
        [w_in[..., :o3], _pad_cols(w_in[..., o3:], MLA_NOPE_DIM, HEAD_PAD - MLA_QK_DIM)], axis=-1)
    wuq = _pad_cols(mla_w_uq.reshape(depth, MLA_Q_LORA, MLA_HEADS, MLA_QK_DIM), 0,
                    HEAD_PAD - MLA_QK_DIM).reshape(depth, MLA_Q_LORA, QK_PAD)
    wukv = mla_w_ukv.reshape(depth, MLA_KV_LORA, MLA_HEADS, MLA_NOPE_DIM + MLA_V_DIM)
    wuk = _pad_cols(wukv[..., :MLA_NOPE_DIM], 0, HEAD_PAD - MLA_NOPE_DIM).reshape(
        depth, MLA_KV_LORA, QK_PAD)
    wuv = _pad_cols(wukv[..., MLA_NOPE_DIM:], 0, HEAD_PAD - MLA_V_DIM).reshape(
        depth, MLA_KV_LORA, QK_PAD)
    kt, winj, wout, lam16 = jax.vmap(_s5_matrices)(
        ssm_a_re, ssm_a_im, ssm_log_dt, ssm_b_re, ssm_b_im, ssm_c_re, ssm_c_im)
    nwin = len(POOL_WINDOWS)
    eye = jnp.eye(nwin, dtype=F32)
    pool_bd = jnp.einsum('ab,lacd->lacbd', eye, pool_w).reshape(depth, POOL_WIDTH, POOL_WIDTH)
    prm = dict(
        g_mix=norm_mix_g[:, None, :], win=win.astype(BF16),
        qg=mla_q_norm_g[:, None, :], wuq=wuq.astype(BF16),
        kvg=mla_kv_norm_g[:, None, :], wuk=wuk.astype(BF16), wuv=wuv.astype(BF16),
        kt=kt, winj=winj, wout=wout, lam16=lam16,
        pool_w=pool_bd.astype(BF16), pool_scale=pool_scale[:, None, :],
        ssm_d=ssm_d[:, None, :], glu_w=ssm_glu_w.astype(BF16), glu_b=ssm_glu_b[:, None, :],
        w_out=w_out.astype(BF16), g_mlp=norm_mlp_g[:, None, :],
        up=mlp_w_up.astype(BF16), dn=mlp_w_down.astype(BF16),
        final_g=final_norm_g[None, :],
    )
    return (_trunk(x_prompt, prm, depth), _trunk(x_sample, prm, depth))
```

```python
import functools
import math

import jax
import jax.numpy as jnp
from jax import lax
from jax.experimental import pallas as pl
from jax.experimental.pallas import tpu as pltpu

F32 = jnp.float32
BF16 = jnp.bfloat16

D_MODEL = 1024
SSM_WIDTH = 256
SSM_GROUP = 16
SSM_GROUPS = 16
SSM_STATE = 64
POOL_WIDTH = 256
POOL_WINDOWS = (2, 4, 8, 16)
POOL_GROUP = 64
MLA_HEADS = 8
MLA_V_DIM = 64
MLA_NOPE_DIM = 64
MLA_ROPE_DIM = 32
MLA_QK_DIM = MLA_NOPE_DIM + MLA_ROPE_DIM
MLA_Q_LORA = 256
MLA_KV_LORA = 128
MLA_WIDTH = MLA_HEADS * MLA_V_DIM
MLA_SCALE = MLA_QK_DIM ** -0.5
Q_PRESCALE = MLA_SCALE * math.log2(math.e)
ROPE_BASE = 10000.0
D_FF = 4 * D_MODEL
NORM_EPS = 1e-6

LANES = 128
SUBLANES = 8
MXU_DIM = 256
VMEM_BYTES = 64 * 1024 * 1024

HEAD_PAD = LANES
QK_PAD = MLA_HEADS * HEAD_PAD
S5_CHUNK = MXU_DIM // SSM_GROUP
S5_PAIR = 2 * SSM_STATE
S5_PAIRS = SSM_GROUPS // 2
S5_PAIR_COLS = 2 * MXU_DIM
POOL_HALO = max(POOL_WINDOWS)
IN_COLS_PAD = SSM_WIDTH + POOL_WIDTH + MLA_Q_LORA + MLA_KV_LORA + HEAD_PAD
FF_CHUNK = 1024
INPROJ_ROWS = 1024
INPROJ_SUB_ROWS = 256
S5_TOKENS_PER_STEP = 8192
ATTN_SCORE_BYTES = 4 * 1024 * 1024
ATTN_KV_BYTES = 16 * 1024 * 1024


def _vmem_limit(block_bytes):
    return int(min(VMEM_BYTES - 8 * 1024 * 1024, block_bytes))


def _rms(x, g):
    return x * lax.rsqrt(jnp.mean(x * x, axis=-1, keepdims=True) + NORM_EPS) * g


def _const_spec(shape):
    nd = len(shape)
    return pl.BlockSpec(shape, lambda *_: (0,) * nd, pipeline_mode=pl.Buffered(1))


def _layer_spec(shape, layer):
    nd = len(shape) - 1
    return pl.BlockSpec((None,) + tuple(shape[1:]), lambda *_: (layer,) + (0,) * nd,
                        pipeline_mode=pl.Buffered(1))


def _layer_bytes(*arrs):
    return sum(a.dtype.itemsize * math.prod(a.shape[1:]) for a in arrs)


def _inproj_kernel(x_ref, g_ref, win_ref, qg_ref, wuq_ref, kvg_ref, wuk_ref, wuv_ref,
                   cos_ref, sa_ref, sb_ref, u_ref, p_ref, q_ref, k_ref, v_ref, *, sub):
    o0 = SSM_WIDTH
    o1 = o0 + POOL_WIDTH
    o2 = o1 + MLA_Q_LORA
    o3 = o2 + MLA_KV_LORA
    nsub = x_ref.shape[0] // sub

    def front(i):
        rs = slice(i * sub, (i + 1) * sub)
        h = _rms(x_ref[rs, :], g_ref[...]).astype(BF16)
        return jnp.dot(h, win_ref[...], preferred_element_type=F32)

    def back(i, proj):
        rs = slice(i * sub, (i + 1) * sub)
        u_ref[rs, :] = proj[:, :o0]
        p_ref[rs, :] = proj[:, o0:o1]
        cos = cos_ref[rs, :]
        sa = sa_ref[rs, :]
        sb = sb_ref[rs, :]

        def rope(t):
            return (t * cos + pltpu.roll(t, HEAD_PAD - MLA_ROPE_DIM // 2, 1) * sa
                    + pltpu.roll(t, MLA_ROPE_DIM // 2, 1) * sb)

        qn = _rms(proj[:, o1:o2], qg_ref[...]).astype(BF16)
        q = jnp.dot(qn, wuq_ref[...], preferred_element_type=F32)
        for hd in range(MLA_HEADS):
            sl = slice(hd * HEAD_PAD, (hd + 1) * HEAD_PAD)
            q_ref[rs, sl] = (rope(q[:, sl]) * Q_PRESCALE).astype(BF16)
        kn = _rms(proj[:, o2:o3], kvg_ref[...]).astype(BF16)
        kk = jnp.dot(kn, wuk_ref[...], preferred_element_type=F32)
        kr = rope(proj[:, o3:])
        for hd in range(MLA_HEADS):
            sl = slice(hd * HEAD_PAD, (hd + 1) * HEAD_PAD)
            k_ref[rs, sl] = (kk[:, sl] + kr).astype(BF16)
        vv = jnp.dot(kn, wuv_ref[...], preferred_element_type=F32)
        lane = lax.broadcasted_iota(jnp.int32, vv.shape, 1)
        v_ref[rs, :] = jnp.where((lane & (HEAD_PAD - 1)) == MLA_V_DIM, 1.0, vv).astype(BF16)

    prev = None
    for i in range(nsub + 1):
        cur = front(i) if i < nsub else None
        if prev is not None:
            back(i - 1, prev)
        prev = cur


def _inproj(x2d, seq, layer, g, win, qg, wuq, kvg, wuk, wuv, cos_t, sa_t, sb_t, *, tm, sub):
    n = x2d.shape[0]
    nlb = seq // tm
    row = lambda i: (i, 0)
    tab = lambda i: (lax.rem(i, nlb), 0)
    out_shapes = (
        jax.ShapeDtypeStruct((n, SSM_WIDTH), F32),
        jax.ShapeDtypeStruct((n, POOL_WIDTH), F32),
        jax.ShapeDtypeStruct((n, QK_PAD), BF16),
        jax.ShapeDtypeStruct((n, QK_PAD), BF16),
        jax.ShapeDtypeStruct((n, QK_PAD), BF16),
    )
    vmem = (2 * tm * D_MODEL * 4 + 2 * tm * (2 * 256 * 4 + 3 * QK_PAD * 2)
            + _layer_bytes(win, wuq, wuk, wuv) + 6 * tm * HEAD_PAD * 4
            + 8 * tm * D_MODEL * 4)
    return pl.pallas_call(
        functools.partial(_inproj_kernel, sub=sub),
        grid=(n // tm,),
        in_specs=[
            pl.BlockSpec((tm, D_MODEL), row),
            _layer_spec(g.shape, layer), _layer_spec(win.shape, layer),
            _layer_spec(qg.shape, layer), _layer_spec(wuq.shape, layer),
            _layer_spec(kvg.shape, layer), _layer_spec(wuk.shape, layer),
            _layer_spec(wuv.shape, layer),
            pl.BlockSpec((tm, HEAD_PAD), tab), pl.BlockSpec((tm, HEAD_PAD), tab),
            pl.BlockSpec((tm, HEAD_PAD), tab),
        ],
        out_specs=(
            pl.BlockSpec((tm, SSM_WIDTH), row), pl.BlockSpec((tm, POOL_WIDTH), row),
            pl.BlockSpec((tm, QK_PAD), row), pl.BlockSpec((tm, QK_PAD), row),
            pl.BlockSpec((tm, QK_PAD), row),
        ),
        out_shape=out_shapes,
        compiler_params=pltpu.CompilerParams(
            dimension_semantics=("parallel",), vmem_limit_bytes=_vmem_limit(vmem)),
        name="inproj",
    )(x2d, g, win, qg, wuq, kvg, wuk, wuv, cos_t, sa_t, sb_t)


GRANULES = LANES // SSM_GROUP
PAIRS_PER_TILE = GRANULES // 2


def _granule_swap_matrix():
    idx = jnp.arange(GRANULES * LANES)
    r, q, h = idx // LANES, (idx // SSM_GROUP) % GRANULES, idx % SSM_GROUP
    return (idx[None, :] == (q * LANES + r * SSM_GROUP + h)[:, None]).astype(BF16)


def _s5_kernel(u_ref, perm_ref, kt_ref, winj_ref, wout_ref, lam_ref, y_ref, uc_ref, yc_ref,
               inj_ref, st_ref, *, nseq, chunks, pitch):
    rows = nseq * chunks
    perm = perm_ref[...]

    for half in range(2):
        zs = jnp.concatenate(
            [u_ref[pl.ds(GRANULES * half + r, rows, stride=S5_CHUNK), :].astype(BF16)
             for r in range(GRANULES)], axis=1)
        us = jnp.dot(zs, perm, preferred_element_type=F32).astype(BF16)
        for q in range(GRANULES):
            col = (q % 2) * MXU_DIM + half * LANES
            uc_ref[q // 2, :, col:col + LANES] = us[:, q * LANES:(q + 1) * LANES]

    if nseq < SUBLANES:
        inj_ref[:, nseq * pitch:, :] = jnp.zeros((4, (SUBLANES - nseq) * pitch, LANES), F32)
    zero = jnp.zeros((SUBLANES, LANES), F32)

    for p in range(PAIRS_PER_TILE):
        uv = uc_ref[p]
        inj = jnp.dot(uv, winj_ref[p], preferred_element_type=F32)
        for b in range(nseq):
            for part in range(4):
                inj_ref[part, b * pitch:b * pitch + chunks, :] = \
                    inj[b * chunks:(b + 1) * chunks, part * LANES:(part + 1) * LANES]
        lam = lam_ref[p]
        lf_re = jnp.broadcast_to(lam[0:1], (SUBLANES, LANES))
        lf_im = jnp.broadcast_to(lam[1:2], (SUBLANES, LANES))
        lb_re = jnp.broadcast_to(lam[2:3], (SUBLANES, LANES))
        lb_im = jnp.broadcast_to(lam[3:4], (SUBLANES, LANES))

        def step(k, carry):
            xf_re, xf_im, xb_re, xb_im = carry
            cf = pl.ds(k, SUBLANES, stride=pitch)
            cb = pl.ds(chunks - 1 - k, SUBLANES, stride=pitch)
            st_ref[0, cf, :] = xf_re
            st_ref[1, cf, :] = xf_im
            st_ref[2, cb, :] = xb_re
            st_ref[3, cb, :] = xb_im
            nf_re = lf_re * xf_re - lf_im * xf_im + inj_ref[0, cf, :]
            nf_im = lf_re * xf_im + lf_im * xf_re + inj_ref[1, cf, :]
            nb_re = lb_re * xb_re - lb_im * xb_im + inj_ref[2, cb, :]
            nb_im = lb_re * xb_im + lb_im * xb_re + inj_ref[3, cb, :]
            return nf_re, nf_im, nb_re, nb_im

        lax.fori_loop(0, chunks, step, (zero, zero, zero, zero))
        st = jnp.concatenate(
            [jnp.concatenate([st_ref[part, b * pitch:b * pitch + chunks, :] for part in range(4)],
                             axis=1) for b in range(nseq)], axis=0)
        y_carry = jnp.dot(st.astype(BF16), wout_ref[p], preferred_element_type=F32)
        y0 = jnp.dot(uv[:, :MXU_DIM], kt_ref[p, 0], preferred_element_type=F32)
        y1 = jnp.dot(uv[:, MXU_DIM:], kt_ref[p, 1], preferred_element_type=F32)
        yc_ref[p, :, :MXU_DIM] = y0 + y_carry[:, :MXU_DIM]
        yc_ref[p, :, MXU_DIM:] = y1 + y_carry[:, MXU_DIM:]

    for half in range(2):
        ys = jnp.concatenate(
            [yc_ref[q // 2, :, (q % 2) * MXU_DIM + half * LANES:(q % 2) * MXU_DIM + (half + 1) * LANES]
             for q in range(GRANULES)], axis=1)
        hi = ys.astype(BF16)
        rest = ys - hi.astype(F32)
        mid = rest.astype(BF16)
        lo = (rest - mid.astype(F32)).astype(BF16)
        zs = (jnp.dot(hi, perm, preferred_element_type=F32)
              + jnp.dot(mid, perm, preferred_element_type=F32)
              + jnp.dot(lo, perm, preferred_element_type=F32))
        for r in range(GRANULES):
            y_ref[pl.ds(GRANULES * half + r, rows, stride=S5_CHUNK), :] = zs[:, r * LANES:(r + 1) * LANES]


def _s5(u2d, seq, layer, perm, kt, winj, wout, lam16, *, nseq):
    n = u2d.shape[0]
    chunks = seq // S5_CHUNK
    rows = nseq * chunks
    pitch = chunks + SUBLANES
    blk = nseq * seq
    cols = S5_PAIR_COLS
    vmem = (4 * blk * LANES * 4 + PAIRS_PER_TILE * rows * cols * (2 + 4)
            + 2 * 4 * SUBLANES * pitch * LANES * 4 + 2 * perm.size
            + 2 * 2 * PAIRS_PER_TILE * (2 * MXU_DIM * MXU_DIM + 2 * cols * cols)
            + 8 * rows * GRANULES * LANES * 4)
    tile = lambda bg, a: (layer, a, 0, 0)
    return pl.pallas_call(
        functools.partial(_s5_kernel, nseq=nseq, chunks=chunks, pitch=pitch),
        grid=(n // blk, SSM_WIDTH // LANES),
        in_specs=[
            pl.BlockSpec((blk, LANES), lambda bg, a: (bg, a)),
            _const_spec(perm.shape),
            pl.BlockSpec((None, PAIRS_PER_TILE, 2, MXU_DIM, MXU_DIM),
                         lambda bg, a: (layer, a, 0, 0, 0)),
            pl.BlockSpec((None, PAIRS_PER_TILE, cols, cols), tile),
            pl.BlockSpec((None, PAIRS_PER_TILE, cols, cols), tile),
            pl.BlockSpec((None, PAIRS_PER_TILE, 4, LANES), tile),
        ],
        out_specs=pl.BlockSpec((blk, LANES), lambda bg, a: (bg, a)),
        out_shape=jax.ShapeDtypeStruct((n, SSM_WIDTH), F32),
        scratch_shapes=[pltpu.VMEM((PAIRS_PER_TILE, rows, cols), BF16),
                        pltpu.VMEM((PAIRS_PER_TILE, rows, cols), F32),
                        pltpu.VMEM((4, SUBLANES * pitch, LANES), F32),
                        pltpu.VMEM((4, SUBLANES * pitch, LANES), F32)],
        compiler_params=pltpu.CompilerParams(
            dimension_semantics=("parallel", "parallel"), vmem_limit_bytes=_vmem_limit(vmem)),
        name="s5_mixer",
    )(u2d, perm, kt, winj, wout, lam16)


def _pool_kernel(p_ref, w_ref, sc_ref, o_ref, pad_ref, *, seq, tile):
    width = POOL_WIDTH
    pad_ref[0:POOL_HALO, :] = jnp.zeros((POOL_HALO, width), F32)
    pad_ref[POOL_HALO + seq:POOL_HALO + seq + POOL_HALO, :] = jnp.zeros((POOL_HALO, width), F32)
    pad_ref[POOL_HALO:POOL_HALO + seq, :] = p_ref[...]
    ext = tile + 2 * POOL_HALO
    left_ext = lax.broadcasted_iota(jnp.int32, (ext, LANES), 1) < POOL_GROUP
    left = lax.broadcasted_iota(jnp.int32, (tile, LANES), 1) < POOL_GROUP
    row = lax.broadcasted_iota(jnp.int32, (tile, LANES), 0)
    w = w_ref[...]
    sc = sc_ref[...]

    def shift(x, k):
        return pltpu.roll(x, k % ext, 0)

    for t0 in range(0, seq, tile):
        halves = []
        for half in range(2):
            x = pad_ref[t0:t0 + ext, half * LANES:(half + 1) * LANES]
            w2 = x + shift(x, 1)
            w4 = shift(w2, 1) + shift(w2, -1)
            if half == 0:
                win = jnp.where(left_ext, w2, w4)
                hw = jnp.where(left, POOL_WINDOWS[0] // 2, POOL_WINDOWS[1] // 2)
            else:
                w8 = shift(w4, 2) + shift(w4, -2)
                w16 = shift(w8, 4) + shift(w8, -4)
                win = jnp.where(left_ext, w8, w16)
                hw = jnp.where(left, POOL_WINDOWS[2] // 2, POOL_WINDOWS[3] // 2)
            pos = row + t0
            cnt = (jnp.minimum(pos + hw, seq) - jnp.maximum(pos - hw, 0)).astype(F32)
            halves.append(win[POOL_HALO:POOL_HALO + tile] / cnt - x[POOL_HALO:POOL_HALO + tile])
        pooled = jnp.concatenate(halves, axis=-1).astype(BF16)
        mixed = jnp.dot(pooled, w, preferred_element_type=F32) * sc
        o_ref[t0:t0 + tile, :] = mixed.astype(BF16)


def _pool(p2d, seq, layer, w_bd, scale, *, tile):
    n = p2d.shape[0]
    vmem = 2 * seq * POOL_WIDTH * 4 + 2 * seq * POOL_WIDTH * 2 + (seq + 32) * POOL_WIDTH * 4 \
        + 64 * tile * LANES * 4 + 4 * POOL_WIDTH * POOL_WIDTH
    return pl.pallas_call(
        functools.partial(_pool_kernel, seq=seq, tile=tile),
        grid=(n // seq,),
        in_specs=[pl.BlockSpec((seq, POOL_WIDTH), lambda b: (b, 0)),
                  _layer_spec(w_bd.shape, layer), _layer_spec(scale.shape, layer)],
        out_specs=pl.BlockSpec((seq, POOL_WIDTH), lambda b: (b, 0)),
        out_shape=jax.ShapeDtypeStruct((n, POOL_WIDTH), BF16),
        scratch_shapes=[pltpu.VMEM((seq + 2 * POOL_HALO, POOL_WIDTH), F32)],
        compiler_params=pltpu.CompilerParams(
            dimension_semantics=("parallel",), vmem_limit_bytes=_vmem_limit(vmem)),
        name="pool_mixer",
    )(p2d, w_bd, scale)


def _attn_kernel(q_ref, k_ref, v_ref, o_ref, *, heads):
    def scores(hh):
        q = q_ref[:, hh * HEAD_PAD:(hh + 1) * HEAD_PAD]
        k = k_ref[:, hh * HEAD_PAD:(hh + 1) * HEAD_PAD]
        return lax.dot_general(q, k, (((1,), (1,)), ((), ())), preferred_element_type=F32)

    s_next = scores(0)
    for hh in range(heads):
        s = s_next
        if hh + 1 < heads:
            s_next = scores(hh + 1)
        m = jnp.max(s, axis=-1, keepdims=True)
        e = jnp.exp2(s - m).astype(BF16)
        o = jnp.dot(e, v_ref[:, hh * HEAD_PAD:(hh + 1) * HEAD_PAD], preferred_element_type=F32)
        o_ref[:, hh * MLA_V_DIM:(hh + 1) * MLA_V_DIM] = (
            o[:, :MLA_V_DIM] / o[:, MLA_V_DIM:MLA_V_DIM + 1]).astype(BF16)


def _attention(q, k, v, seq, *, tq, heads):
    n = q.shape[0]
    nb = n // seq
    nq = seq // tq
    vmem = (2 * tq * heads * HEAD_PAD * 2 + 4 * seq * heads * HEAD_PAD * 2
            + 2 * tq * heads * MLA_V_DIM * 2 + 2 * heads * tq * seq * 4)
    return pl.pallas_call(
        functools.partial(_attn_kernel, heads=heads),
        grid=(nb, MLA_HEADS // heads, nq),
        in_specs=[
            pl.BlockSpec((tq, heads * HEAD_PAD), lambda b, hg, i: (b * nq + i, hg)),
            pl.BlockSpec((seq, heads * HEAD_PAD), lambda b, hg, i: (b, hg)),
            pl.BlockSpec((seq, heads * HEAD_PAD), lambda b, hg, i: (b, hg)),
        ],
        out_specs=pl.BlockSpec((tq, heads * MLA_V_DIM), lambda b, hg, i: (b * nq + i, hg)),
        out_shape=jax.ShapeDtypeStruct((n, MLA_WIDTH), BF16),
        compiler_params=pltpu.CompilerParams(
            dimension_semantics=("parallel", "parallel", "parallel"),
            vmem_limit_bytes=_vmem_limit(vmem)),
        name="mla_attention",
    )(q, k, v)


def _outmlp_kernel(x_ref, u_ref, yc_ref, yp_ref, ym_ref, d_ref, gw_ref, gb_ref, wo_ref,
                   g_ref, up_ref, dn_ref, fg_ref, o_ref, *, final):
    y = yc_ref[...] + d_ref[...] * u_ref[...]
    y = jax.nn.gelu(y)
    gate = jnp.dot(y.astype(BF16), gw_ref[...], preferred_element_type=F32) + gb_ref[...]
    y_ssm = (y * jax.nn.sigmoid(gate)).astype(BF16)
    o1 = SSM_WIDTH + POOL_WIDTH
    mix = jnp.dot(y_ssm, wo_ref[0:SSM_WIDTH, :], preferred_element_type=F32)
    mix = mix + jnp.dot(yp_ref[...], wo_ref[SSM_WIDTH:o1, :], preferred_element_type=F32)
    mix = mix + jnp.dot(ym_ref[...], wo_ref[o1:, :], preferred_element_type=F32)
    x1 = x_ref[...] + mix
    hn = _rms(x1, g_ref[...]).astype(BF16)
    acc = jnp.zeros_like(x1)
    for c in range(0, D_FF, FF_CHUNK):
        hc = jnp.dot(hn, up_ref[:, c:c + FF_CHUNK], preferred_element_type=F32)
        hc = jnp.square(jnp.maximum(hc, 0.0)).astype(BF16)
        acc = acc + jnp.dot(hc, dn_ref[c:c + FF_CHUNK, :], preferred_element_type=F32)
    x2 = x1 + acc
    if final:
        x2 = _rms(x2, fg_ref[...])
    o_ref[...] = x2


def _outmlp(x2d, u, yc, yp, ym, layer, d, gw, gb, wo, g, up, dn, fg, *, tm, final):
    n = x2d.shape[0]
    row = lambda i: (i, 0)
    vmem = (4 * tm * D_MODEL * 4 + 2 * tm * (2 * 256 * 4 + 256 * 2 + MLA_WIDTH * 2)
            + _layer_bytes(gw, wo, up, dn)
            + 6 * tm * D_MODEL * 4 + 3 * tm * FF_CHUNK * 4)
    return pl.pallas_call(
        functools.partial(_outmlp_kernel, final=final),
        grid=(n // tm,),
        in_specs=[
            pl.BlockSpec((tm, D_MODEL), row), pl.BlockSpec((tm, SSM_WIDTH), row),
            pl.BlockSpec((tm, SSM_WIDTH), row), pl.BlockSpec((tm, POOL_WIDTH), row),
            pl.BlockSpec((tm, MLA_WIDTH), row),
            _layer_spec(d.shape, layer), _layer_spec(gw.shape, layer), _layer_spec(gb.shape, layer),
            _layer_spec(wo.shape, layer), _layer_spec(g.shape, layer), _layer_spec(up.shape, layer),
            _layer_spec(dn.shape, layer), _const_spec(fg.shape),
        ],
        out_specs=pl.BlockSpec((tm, D_MODEL), row),
        out_shape=jax.ShapeDtypeStruct((n, D_MODEL), F32),
        compiler_params=pltpu.CompilerParams(
            dimension_semantics=("parallel",), vmem_limit_bytes=_vmem_limit(vmem)),
        name="outproj_mlp",
    )(x2d, u, yc, yp, ym, d, gw, gb, wo, g, up, dn, fg)


def _rope_tables(seq):
    half = MLA_ROPE_DIM // 2
    inv_freq = ROPE_BASE ** (-jnp.arange(0, MLA_ROPE_DIM, 2, dtype=F32) / MLA_ROPE_DIM)
    ang = jnp.arange(seq, dtype=F32)[:, None] * inv_freq[None, :]
    cos, sin = jnp.cos(ang), jnp.sin(ang)
    z = jnp.zeros_like(cos)
    ones = jnp.ones((seq, MLA_NOPE_DIM), F32)
    pad = jnp.zeros((seq, HEAD_PAD - MLA_QK_DIM), F32)
    zn = jnp.zeros((seq, MLA_NOPE_DIM), F32)
    cos_t = jnp.concatenate([ones, cos, cos, pad], axis=-1)
    sa_t = jnp.concatenate([zn, -sin, z, pad], axis=-1)
    sb_t = jnp.concatenate([zn, z, sin, pad], axis=-1)
    return cos_t, sa_t, sb_t


def _s5_matrices(a_re, a_im, log_dt, b_re, b_im, c_re, c_im):
    hp = lax.Precision.HIGHEST
    t = S5_CHUNK
    g, p, hch = SSM_GROUPS, SSM_STATE, SSM_GROUP
    dt = jnp.exp(log_dt)[..., None]
    mag = jnp.exp(a_re * dt)
    ang = a_im * dt
    lb_re = mag * jnp.cos(ang)
    lb_im = mag * jnp.sin(ang)
    den = a_re * a_re + a_im * a_im
    f_re = ((lb_re - 1.0) * a_re + lb_im * a_im) / den
    f_im = (lb_im * a_re - (lb_re - 1.0) * a_im) / den
    pw_re = [jnp.ones_like(lb_re)]
    pw_im = [jnp.zeros_like(lb_im)]
    for _ in range(t):
        r, i = pw_re[-1], pw_im[-1]
        pw_re.append(r * lb_re - i * lb_im)
        pw_im.append(r * lb_im + i * lb_re)
    pw_re = jnp.stack(pw_re)
    pw_im = jnp.stack(pw_im)
    bt_re = jnp.swapaxes(b_re, 1, 2)
    bt_im = jnp.swapaxes(b_im, 1, 2)
    fr, fi = f_re[:, :, None, :], f_im[:, :, None, :]
    bb_re = fr * bt_re[None] - fi * bt_im[None]
    bb_im = fr * bt_im[None] + fi * bt_re[None]
    pr, pi = pw_re[:, :, :, None, :], pw_im[:, :, :, None, :]
    w_re = pr * bb_re[None] - pi * bb_im[None]
    w_im = pr * bb_im[None] + pi * bb_re[None]

    def strip(w):
        return jnp.concatenate([w[t - 1:0:-1, 1], (w[0, 0] + w[0, 1])[None], w[1:t, 0]], axis=0)

    lagk = (jnp.einsum('gkp,lghp->ghlk', c_re, strip(w_re), precision=hp)
            - jnp.einsum('gkp,lghp->ghlk', c_im, strip(w_im), precision=hp))
    lagk = lagk.reshape(g, hch, (2 * t - 1) * hch)
    kt = jnp.stack([lagk[:, :, (t - 1 - j) * hch:(t - 1 - j) * hch + t * hch] for j in range(t)],
                   axis=1)
    kt = kt.reshape(S5_PAIRS, 2, MXU_DIM, MXU_DIM)

    def pair_blocks(parts):
        x = jnp.stack(parts, axis=3).reshape(S5_PAIRS, 2, t, hch, 4, p)
        even = jnp.pad(x[:, 0], [(0, 0)] * 4 + [(0, p)])
        odd = jnp.pad(x[:, 1], [(0, 0)] * 4 + [(p, 0)])
        return jnp.stack([even, odd], axis=1).reshape(S5_PAIRS, S5_PAIR_COLS, 4 * S5_PAIR)

    def by_group(w):
        return jnp.swapaxes(w, 0, 1)

    winj = pair_blocks([by_group(w_re[t - 1::-1, 0][:t]), by_group(w_im[t - 1::-1, 0][:t]),
                        by_group(w_re[:t, 1]), by_group(w_im[:t, 1])])

    def c_times(qr, qi):
        qr, qi = jnp.swapaxes(qr, 0, 1)[:, :, None, :], jnp.swapaxes(qi, 0, 1)[:, :, None, :]
        cr, ci = c_re[:, None], c_im[:, None]
        return cr * qr - ci * qi, -(cr * qi + ci * qr)

    of_re, of_im = c_times(pw_re[1:t + 1, 0], pw_im[1:t + 1, 0])
    ob_re, ob_im = c_times(pw_re[t:0:-1, 1], pw_im[t:0:-1, 1])
    wout = jnp.swapaxes(pair_blocks([of_re, of_im, ob_re, ob_im]), 1, 2)
    lam16 = jnp.stack([pw_re[t, 0], pw_im[t, 0], pw_re[t, 1], pw_im[t, 1]])
    lam16 = jnp.transpose(lam16.reshape(4, S5_PAIRS, S5_PAIR), (1, 0, 2))
    return kt.astype(BF16), winj.astype(BF16), wout.astype(BF16), lam16


def _pad_cols(w, lo, hi):
    return jnp.pad(w, [(0, 0)] * (w.ndim - 1) + [(lo, hi)])


def _trunk(x, prm, depth):
    batch, seq, _ = x.shape
    n = batch * seq
    x2d = x.reshape(n, D_MODEL)
    cos_t, sa_t, sb_t = _rope_tables(seq)
    tm = min(512, seq)
    nseq = math.gcd(batch, max(1, min(SUBLANES, S5_TOKENS_PER_STEP // seq)))
    tq = min(seq, ATTN_SCORE_BYTES // (4 * seq))
    heads = max(1, min(MLA_HEADS, ATTN_KV_BYTES // (4 * seq * HEAD_PAD * 2)))
    perm = _granule_swap_matrix()
    w = prm
    for l in range(depth):
        u, p, q, k, v = _inproj(x2d, seq, l, w['g_mix'], w['win'], w['qg'], w['wuq'], w['kvg'],
                                w['wuk'], w['wuv'], cos_t, sa_t, sb_t,
                                tm=min(INPROJ_ROWS, seq), sub=min(INPROJ_SUB_ROWS, seq))
        yc = _s5(u, seq, l, perm, w['kt'], w['winj'], w['wout'], w['lam16'], nseq=nseq)
        yp = _pool(p, seq, l, w['pool_w'], w['pool_scale'], tile=min(256, seq))
        ym = _attention(q, k, v, seq, tq=tq, heads=heads)
        x2d = _outmlp(x2d, u, yc, yp, ym, l, w['ssm_d'], w['glu_w'], w['glu_b'], w['w_out'],
                      w['g_mlp'], w['up'], w['dn'], prm['final_g'], tm=tm,
                      final=(l == depth - 1))
    return x2d.reshape(batch, seq, D_MODEL)


def _prepare(norm_mix_g, w_in, ssm_a_re, ssm_a_im, ssm_log_dt, ssm_b_re, ssm_b_im, ssm_c_re, ssm_c_im,
             ssm_d, ssm_glu_w, ssm_glu_b, pool_w, pool_scale, mla_q_norm_g, mla_w_uq, mla_kv_norm_g,
             mla_w_ukv, w_out, norm_mlp_g, mlp_w_up, mlp_w_down, final_norm_g):
    depth = w_in.shape[0]
    o3 = SSM_WIDTH + POOL_WIDTH + MLA_Q_LORA + MLA_KV_LORA
    win = jnp.concatenate(
        [w_in[..., :o3], _pad_cols(w_in[..., o3:], MLA_NOPE_DIM, HEAD_PAD - MLA_QK_DIM)], axis=-1)
    wuq = _pad_cols(mla_w_uq.reshape(depth, MLA_Q_LORA, MLA_HEADS, MLA_QK_DIM), 0,
                    HEAD_PAD - MLA_QK_DIM).reshape(depth, MLA_Q_LORA, QK_PAD)
    wukv = mla_w_ukv.reshape(depth, MLA_KV_LORA, MLA_HEADS, MLA_NOPE_DIM + MLA_V_DIM)
    wuk = _pad_cols(wukv[..., :MLA_NOPE_DIM], 0, HEAD_PAD - MLA_NOPE_DIM).reshape(
        depth, MLA_KV_LORA, QK_PAD)
    wuv = _pad_cols(wukv[..., MLA_NOPE_DIM:], 0, HEAD_PAD - MLA_V_DIM).reshape(
        depth, MLA_KV_LORA, QK_PAD)
    kt, winj, wout, lam16 = jax.vmap(_s5_matrices)(
        ssm_a_re, ssm_a_im, ssm_log_dt, ssm_b_re, ssm_b_im, ssm_c_re, ssm_c_im)
    nwin = len(POOL_WINDOWS)
    eye = jnp.eye(nwin, dtype=F32)
    pool_bd = jnp.einsum('ab,lacd->lacbd', eye, pool_w).reshape(depth, POOL_WIDTH, POOL_WIDTH)
    prm = dict(
        g_mix=norm_mix_g[:, None, :], win=win.astype(BF16),
        qg=mla_q_norm_g[:, None, :], wuq=wuq.astype(BF16),
        kvg=mla_kv_norm_g[:, None, :], wuk=wuk.astype(BF16), wuv=wuv.astype(BF16),
        kt=kt, winj=winj, wout=wout, lam16=lam16,
        pool_w=pool_bd.astype(BF16), pool_scale=pool_scale[:, None, :],
        ssm_d=ssm_d[:, None, :], glu_w=ssm_glu_w.astype(BF16), glu_b=ssm_glu_b[:, None, :],
        w_out=w_out.astype(BF16), g_mlp=norm_mlp_g[:, None, :],
        up=mlp_w_up.astype(BF16), dn=mlp_w_down.astype(BF16),
        final_g=final_norm_g[None, :],
    )
    return prm


def kernel(x_prompt, x_sample, norm_mix_g, w_in, ssm_a_re, ssm_a_im, ssm_log_dt, ssm_b_re, ssm_b_im, ssm_c_re, ssm_c_im, ssm_d, ssm_glu_w, ssm_glu_b, pool_w, pool_scale, mla_q_norm_g, mla_w_uq, mla_kv_norm_g, mla_w_ukv, w_out, norm_mlp_g, mlp_w_up, mlp_w_down, final_norm_g):
    depth = w_in.shape[0]
    prm = _prepare(norm_mix_g, w_in, ssm_a_re, ssm_a_im, ssm_log_dt, ssm_b_re, ssm_b_im, ssm_c_re,
                   ssm_c_im, ssm_d, ssm_glu_w, ssm_glu_b, pool_w, pool_scale, mla_q_norm_g, mla_w_uq,
                   mla_kv_norm_g, mla_w_ukv, w_out, norm_mlp_g, mlp_w_up, mlp_w_down, final_norm_g)
    return (_trunk(x_prompt, prm, depth), _trunk(x_sample, prm, depth))
```

```python
import functools
import math

import jax
import jax.numpy as jnp
from jax import lax
from jax.experimental import pallas as pl
from jax.experimental.pallas import tpu as pltpu

F32 = jnp.float32
BF16 = jnp.bfloat16

D_MODEL = 1024
SSM_WIDTH = 256
SSM_GROUP = 16
SSM_GROUPS = 16
SSM_STATE = 64
POOL_WIDTH = 256
POOL_WINDOWS = (2, 4, 8, 16)
POOL_GROUP = 64
MLA_HEADS = 8
MLA_V_DIM = 64
MLA_NOPE_DIM = 64
MLA_ROPE_DIM = 32
MLA_QK_DIM = MLA_NOPE_DIM + MLA_ROPE_DIM
MLA_Q_LORA = 256
MLA_KV_LORA = 128
MLA_WIDTH = MLA_HEADS * MLA_V_DIM
MLA_SCALE = MLA_QK_DIM ** -0.5
Q_PRESCALE = MLA_SCALE * math.log2(math.e)
ROPE_BASE = 10000.0
D_FF = 4 * D_MODEL
NORM_EPS = 1e-6

LANES = 128
SUBLANES = 8
MXU_DIM = 256
VMEM_BYTES = 64 * 1024 * 1024

HEAD_PAD = LANES
QK_PAD = MLA_HEADS * HEAD_PAD
S5_CHUNK = MXU_DIM // SSM_GROUP
S5_PAIR = 2 * SSM_STATE
S5_PAIRS = SSM_GROUPS // 2
S5_PAIR_COLS = 2 * MXU_DIM
POOL_HALO = max(POOL_WINDOWS)
IN_COLS_PAD = SSM_WIDTH + POOL_WIDTH + MLA_Q_LORA + MLA_KV_LORA + HEAD_PAD
FF_CHUNK = 1024
INPROJ_ROWS = 1024
INPROJ_SUB_ROWS = 256
OUTMLP_ROWS = 1024
S5_TOKENS_PER_STEP = 8192
ATTN_SCORE_BYTES = 4 * 1024 * 1024
ATTN_KV_BYTES = 16 * 1024 * 1024


def _vmem_limit(block_bytes):
    return int(min(VMEM_BYTES - 8 * 1024 * 1024, block_bytes))


def _rms(x, g):
    return x * lax.rsqrt(jnp.mean(x * x, axis=-1, keepdims=True) + NORM_EPS) * g


def _const_spec(shape):
    nd = len(shape)
    return pl.BlockSpec(shape, lambda *_: (0,) * nd, pipeline_mode=pl.Buffered(1))


def _layer_spec(shape, layer):
    nd = len(shape) - 1
    return pl.BlockSpec((None,) + tuple(shape[1:]), lambda *_: (layer,) + (0,) * nd,
                        pipeline_mode=pl.Buffered(1))


def _layer_bytes(*arrs):
    return sum(a.dtype.itemsize * math.prod(a.shape[1:]) for a in arrs)


def _inproj_kernel(x_ref, g_ref, win_ref, qg_ref, wuq_ref, kvg_ref, wuk_ref, wuv_ref,
                   cos_ref, sa_ref, sb_ref, u_ref, p_ref, q_ref, k_ref, v_ref, *, sub):
    o0 = SSM_WIDTH
    o1 = o0 + POOL_WIDTH
    o2 = o1 + MLA_Q_LORA
    o3 = o2 + MLA_KV_LORA
    nsub = x_ref.shape[0] // sub

    def front(i):
        rs = slice(i * sub, (i + 1) * sub)
        h = _rms(x_ref[rs, :], g_ref[...]).astype(BF16)
        return jnp.dot(h, win_ref[...], preferred_element_type=F32)

    def back(i, proj):
        rs = slice(i * sub, (i + 1) * sub)
        u_ref[rs, :] = proj[:, :o0]
        p_ref[rs, :] = proj[:, o0:o1]
        cos = cos_ref[rs, :]
        sa = sa_ref[rs, :]
        sb = sb_ref[rs, :]

        def rope(t):
            return (t * cos + pltpu.roll(t, HEAD_PAD - MLA_ROPE_DIM // 2, 1) * sa
                    + pltpu.roll(t, MLA_ROPE_DIM // 2, 1) * sb)

        qn = _rms(proj[:, o1:o2], qg_ref[...]).astype(BF16)
        q = jnp.dot(qn, wuq_ref[...], preferred_element_type=F32)
        for hd in range(MLA_HEADS):
            sl = slice(hd * HEAD_PAD, (hd + 1) * HEAD_PAD)
            q_ref[rs, sl] = (rope(q[:, sl]) * Q_PRESCALE).astype(BF16)
        kn = _rms(proj[:, o2:o3], kvg_ref[...]).astype(BF16)
        kk = jnp.dot(kn, wuk_ref[...], preferred_element_type=F32)
        kr = rope(proj[:, o3:])
        for hd in range(MLA_HEADS):
            sl = slice(hd * HEAD_PAD, (hd + 1) * HEAD_PAD)
            k_ref[rs, sl] = (kk[:, sl] + kr).astype(BF16)
        vv = jnp.dot(kn, wuv_ref[...], preferred_element_type=F32)
        lane = lax.broadcasted_iota(jnp.int32, vv.shape, 1)
        v_ref[rs, :] = jnp.where((lane & (HEAD_PAD - 1)) == MLA_V_DIM, 1.0, vv).astype(BF16)

    prev = None
    for i in range(nsub + 1):
        cur = front(i) if i < nsub else None
        if prev is not None:
            back(i - 1, prev)
        prev = cur


def _inproj(x2d, seq, layer, g, win, qg, wuq, kvg, wuk, wuv, cos_t, sa_t, sb_t, *, tm, sub):
    n = x2d.shape[0]
    nlb = seq // tm
    row = lambda i: (i, 0)
    tab = lambda i: (lax.rem(i, nlb), 0)
    out_shapes = (
        jax.ShapeDtypeStruct((n, SSM_WIDTH), F32),
        jax.ShapeDtypeStruct((n, POOL_WIDTH), F32),
        jax.ShapeDtypeStruct((n, QK_PAD), BF16),
        jax.ShapeDtypeStruct((n, QK_PAD), BF16),
        jax.ShapeDtypeStruct((n, QK_PAD), BF16),
    )
    vmem = (2 * tm * D_MODEL * 4 + 2 * tm * (2 * 256 * 4 + 3 * QK_PAD * 2)
            + _layer_bytes(win, wuq, wuk, wuv) + 6 * tm * HEAD_PAD * 4
            + 8 * tm * D_MODEL * 4)
    return pl.pallas_call(
        functools.partial(_inproj_kernel, sub=sub),
        grid=(n // tm,),
        in_specs=[
            pl.BlockSpec((tm, D_MODEL), row),
            _layer_spec(g.shape, layer), _layer_spec(win.shape, layer),
            _layer_spec(qg.shape, layer), _layer_spec(wuq.shape, layer),
            _layer_spec(kvg.shape, layer), _layer_spec(wuk.shape, layer),
            _layer_spec(wuv.shape, layer),
            pl.BlockSpec((tm, HEAD_PAD), tab), pl.BlockSpec((tm, HEAD_PAD), tab),
            pl.BlockSpec((tm, HEAD_PAD), tab),
        ],
        out_specs=(
            pl.BlockSpec((tm, SSM_WIDTH), row), pl.BlockSpec((tm, POOL_WIDTH), row),
            pl.BlockSpec((tm, QK_PAD), row), pl.BlockSpec((tm, QK_PAD), row),
            pl.BlockSpec((tm, QK_PAD), row),
        ),
        out_shape=out_shapes,
        compiler_params=pltpu.CompilerParams(
            dimension_semantics=("parallel",), vmem_limit_bytes=_vmem_limit(vmem)),
        name="inproj",
    )(x2d, g, win, qg, wuq, kvg, wuk, wuv, cos_t, sa_t, sb_t)


GRANULES = LANES // SSM_GROUP
PAIRS_PER_TILE = GRANULES // 2


def _granule_swap_matrix():
    idx = jnp.arange(GRANULES * LANES)
    r, q, h = idx // LANES, (idx // SSM_GROUP) % GRANULES, idx % SSM_GROUP
    return (idx[None, :] == (q * LANES + r * SSM_GROUP + h)[:, None]).astype(BF16)


def _s5_kernel(u_ref, perm_ref, kt_ref, winj_ref, wout_ref, lam_ref, y_ref, uc_ref, yc_ref,
               inj_ref, st_ref, *, nseq, chunks, pitch):
    rows = nseq * chunks
    perm = perm_ref[...]

    for half in range(2):
        zs = jnp.concatenate(
            [u_ref[pl.ds(GRANULES * half + r, rows, stride=S5_CHUNK), :].astype(BF16)
             for r in range(GRANULES)], axis=1)
        us = jnp.dot(zs, perm, preferred_element_type=F32).astype(BF16)
        for q in range(GRANULES):
            col = (q % 2) * MXU_DIM + half * LANES
            uc_ref[q // 2, :, col:col + LANES] = us[:, q * LANES:(q + 1) * LANES]

    if nseq < SUBLANES:
        inj_ref[:, nseq * pitch:, :] = jnp.zeros((4, (SUBLANES - nseq) * pitch, LANES), F32)
    zero = jnp.zeros((SUBLANES, LANES), F32)

    for p in range(PAIRS_PER_TILE):
        uv = uc_ref[p]
        inj = jnp.dot(uv, winj_ref[p], preferred_element_type=F32)
        for b in range(nseq):
            for part in range(4):
                inj_ref[part, b * pitch:b * pitch + chunks, :] = \
                    inj[b * chunks:(b + 1) * chunks, part * LANES:(part + 1) * LANES]
        lam = lam_ref[p]
        lf_re = jnp.broadcast_to(lam[0:1], (SUBLANES, LANES))
        lf_im = jnp.broadcast_to(lam[1:2], (SUBLANES, LANES))
        lb_re = jnp.broadcast_to(lam[2:3], (SUBLANES, LANES))
        lb_im = jnp.broadcast_to(lam[3:4], (SUBLANES, LANES))

        def step(k, carry):
            xf_re, xf_im, xb_re, xb_im = carry
            cf = pl.ds(k, SUBLANES, stride=pitch)
            cb = pl.ds(chunks - 1 - k, SUBLANES, stride=pitch)
            st_ref[0, cf, :] = xf_re
            st_ref[1, cf, :] = xf_im
            st_ref[2, cb, :] = xb_re
            st_ref[3, cb, :] = xb_im
            nf_re = lf_re * xf_re - lf_im * xf_im + inj_ref[0, cf, :]
            nf_im = lf_re * xf_im + lf_im * xf_re + inj_ref[1, cf, :]
            nb_re = lb_re * xb_re - lb_im * xb_im + inj_ref[2, cb, :]
            nb_im = lb_re * xb_im + lb_im * xb_re + inj_ref[3, cb, :]
            return nf_re, nf_im, nb_re, nb_im

        lax.fori_loop(0, chunks, step, (zero, zero, zero, zero))
        st = jnp.concatenate(
            [jnp.concatenate([st_ref[part, b * pitch:b * pitch + chunks, :] for part in range(4)],
                             axis=1) for b in range(nseq)], axis=0)
        y_carry = jnp.dot(st.astype(BF16), wout_ref[p], preferred_element_type=F32)
        y0 = jnp.dot(uv[:, :MXU_DIM], kt_ref[p, 0], preferred_element_type=F32)
        y1 = jnp.dot(uv[:, MXU_DIM:], kt_ref[p, 1], preferred_element_type=F32)
        yc_ref[p, :, :MXU_DIM] = y0 + y_carry[:, :MXU_DIM]
        yc_ref[p, :, MXU_DIM:] = y1 + y_carry[:, MXU_DIM:]

    for half in range(2):
        ys = jnp.concatenate(
            [yc_ref[q // 2, :, (q % 2) * MXU_DIM + half * LANES:(q % 2) * MXU_DIM + (half + 1) * LANES]
             for q in range(GRANULES)], axis=1)
        hi = ys.astype(BF16)
        rest = ys - hi.astype(F32)
        mid = rest.astype(BF16)
        lo = (rest - mid.astype(F32)).astype(BF16)
        zs = (jnp.dot(hi, perm, preferred_element_type=F32)
              + jnp.dot(mid, perm, preferred_element_type=F32)
              + jnp.dot(lo, perm, preferred_element_type=F32))
        for r in range(GRANULES):
            y_ref[pl.ds(GRANULES * half + r, rows, stride=S5_CHUNK), :] = zs[:, r * LANES:(r + 1) * LANES]


def _s5(u2d, seq, layer, perm, kt, winj, wout, lam16, *, nseq):
    n = u2d.shape[0]
    chunks = seq // S5_CHUNK
    rows = nseq * chunks
    pitch = chunks + SUBLANES
    blk = nseq * seq
    cols = S5_PAIR_COLS
    vmem = (4 * blk * LANES * 4 + PAIRS_PER_TILE * rows * cols * (2 + 4)
            + 2 * 4 * SUBLANES * pitch * LANES * 4 + 2 * perm.size
            + 2 * 2 * PAIRS_PER_TILE * (2 * MXU_DIM * MXU_DIM + 2 * cols * cols)
            + 8 * rows * GRANULES * LANES * 4)
    tile = lambda bg, a: (layer, a, 0, 0)
    return pl.pallas_call(
        functools.partial(_s5_kernel, nseq=nseq, chunks=chunks, pitch=pitch),
        grid=(n // blk, SSM_WIDTH // LANES),
        in_specs=[
            pl.BlockSpec((blk, LANES), lambda bg, a: (bg, a)),
            _const_spec(perm.shape),
            pl.BlockSpec((None, PAIRS_PER_TILE, 2, MXU_DIM, MXU_DIM),
                         lambda bg, a: (layer, a, 0, 0, 0)),
            pl.BlockSpec((None, PAIRS_PER_TILE, cols, cols), tile),
            pl.BlockSpec((None, PAIRS_PER_TILE, cols, cols), tile),
            pl.BlockSpec((None, PAIRS_PER_TILE, 4, LANES), tile),
        ],
        out_specs=pl.BlockSpec((blk, LANES), lambda bg, a: (bg, a)),
        out_shape=jax.ShapeDtypeStruct((n, SSM_WIDTH), F32),
        scratch_shapes=[pltpu.VMEM((PAIRS_PER_TILE, rows, cols), BF16),
                        pltpu.VMEM((PAIRS_PER_TILE, rows, cols), F32),
                        pltpu.VMEM((4, SUBLANES * pitch, LANES), F32),
                        pltpu.VMEM((4, SUBLANES * pitch, LANES), F32)],
        compiler_params=pltpu.CompilerParams(
            dimension_semantics=("parallel", "parallel"), vmem_limit_bytes=_vmem_limit(vmem)),
        name="s5_mixer",
    )(u2d, perm, kt, winj, wout, lam16)


def _pool_kernel(p_ref, w_ref, sc_ref, o_ref, pad_ref, *, seq, tile):
    width = POOL_WIDTH
    pad_ref[0:POOL_HALO, :] = jnp.zeros((POOL_HALO, width), F32)
    pad_ref[POOL_HALO + seq:POOL_HALO + seq + POOL_HALO, :] = jnp.zeros((POOL_HALO, width), F32)
    pad_ref[POOL_HALO:POOL_HALO + seq, :] = p_ref[...]
    ext = tile + 2 * POOL_HALO
    left_ext = lax.broadcasted_iota(jnp.int32, (ext, LANES), 1) < POOL_GROUP
    left = lax.broadcasted_iota(jnp.int32, (tile, LANES), 1) < POOL_GROUP
    row = lax.broadcasted_iota(jnp.int32, (tile, LANES), 0)
    w = w_ref[...]
    sc = sc_ref[...]

    def shift(x, k):
        return pltpu.roll(x, k % ext, 0)

    for t0 in range(0, seq, tile):
        halves = []
        for half in range(2):
            x = pad_ref[t0:t0 + ext, half * LANES:(half + 1) * LANES]
            w2 = x + shift(x, 1)
            w4 = shift(w2, 1) + shift(w2, -1)
            if half == 0:
                win = jnp.where(left_ext, w2, w4)
                hw = jnp.where(left, POOL_WINDOWS[0] // 2, POOL_WINDOWS[1] // 2)
            else:
                w8 = shift(w4, 2) + shift(w4, -2)
                w16 = shift(w8, 4) + shift(w8, -4)
                win = jnp.where(left_ext, w8, w16)
                hw = jnp.where(left, POOL_WINDOWS[2] // 2, POOL_WINDOWS[3] // 2)
            pos = row + t0
            cnt = (jnp.minimum(pos + hw, seq) - jnp.maximum(pos - hw, 0)).astype(F32)
            halves.append(win[POOL_HALO:POOL_HALO + tile] / cnt - x[POOL_HALO:POOL_HALO + tile])
        pooled = jnp.concatenate(halves, axis=-1).astype(BF16)
        mixed = jnp.dot(pooled, w, preferred_element_type=F32) * sc
        o_ref[t0:t0 + tile, :] = mixed.astype(BF16)


def _pool(p2d, seq, layer, w_bd, scale, *, tile):
    n = p2d.shape[0]
    vmem = 2 * seq * POOL_WIDTH * 4 + 2 * seq * POOL_WIDTH * 2 + (seq + 32) * POOL_WIDTH * 4 \
        + 64 * tile * LANES * 4 + 4 * POOL_WIDTH * POOL_WIDTH
    return pl.pallas_call(
        functools.partial(_pool_kernel, seq=seq, tile=tile),
        grid=(n // seq,),
        in_specs=[pl.BlockSpec((seq, POOL_WIDTH), lambda b: (b, 0)),
                  _layer_spec(w_bd.shape, layer), _layer_spec(scale.shape, layer)],
        out_specs=pl.BlockSpec((seq, POOL_WIDTH), lambda b: (b, 0)),
        out_shape=jax.ShapeDtypeStruct((n, POOL_WIDTH), BF16),
        scratch_shapes=[pltpu.VMEM((seq + 2 * POOL_HALO, POOL_WIDTH), F32)],
        compiler_params=pltpu.CompilerParams(
            dimension_semantics=("parallel",), vmem_limit_bytes=_vmem_limit(vmem)),
        name="pool_mixer",
    )(p2d, w_bd, scale)


def _attn_kernel(q_ref, k_ref, v_ref, o_ref, *, heads):
    def scores(hh):
        q = q_ref[:, hh * HEAD_PAD:(hh + 1) * HEAD_PAD]
        k = k_ref[:, hh * HEAD_PAD:(hh + 1) * HEAD_PAD]
        return lax.dot_general(q, k, (((1,), (1,)), ((), ())), preferred_element_type=F32)

    s_next = scores(0)
    for hh in range(heads):
        s = s_next
        if hh + 1 < heads:
            s_next = scores(hh + 1)
        m = jnp.max(s, axis=-1, keepdims=True)
        e = jnp.exp2(s - m).astype(BF16)
        o = jnp.dot(e, v_ref[:, hh * HEAD_PAD:(hh + 1) * HEAD_PAD], preferred_element_type=F32)
        o_ref[:, hh * MLA_V_DIM:(hh + 1) * MLA_V_DIM] = (
            o[:, :MLA_V_DIM] / o[:, MLA_V_DIM:MLA_V_DIM + 1]).astype(BF16)


def _attention(q, k, v, seq, *, tq, heads):
    n = q.shape[0]
    nb = n // seq
    nq = seq // tq
    vmem = (2 * tq * heads * HEAD_PAD * 2 + 4 * seq * heads * HEAD_PAD * 2
            + 2 * tq * heads * MLA_V_DIM * 2 + 2 * heads * tq * seq * 4)
    return pl.pallas_call(
        functools.partial(_attn_kernel, heads=heads),
        grid=(nb, MLA_HEADS // heads, nq),
        in_specs=[
            pl.BlockSpec((tq, heads * HEAD_PAD), lambda b, hg, i: (b * nq + i, hg)),
            pl.BlockSpec((seq, heads * HEAD_PAD), lambda b, hg, i: (b, hg)),
            pl.BlockSpec((seq, heads * HEAD_PAD), lambda b, hg, i: (b, hg)),
        ],
        out_specs=pl.BlockSpec((tq, heads * MLA_V_DIM), lambda b, hg, i: (b * nq + i, hg)),
        out_shape=jax.ShapeDtypeStruct((n, MLA_WIDTH), BF16),
        compiler_params=pltpu.CompilerParams(
            dimension_semantics=("parallel", "parallel", "parallel"),
            vmem_limit_bytes=_vmem_limit(vmem)),
        name="mla_attention",
    )(q, k, v)


def _outmlp_kernel(x_ref, u_ref, yc_ref, yp_ref, ym_ref, d_ref, gw_ref, gb_ref, wo_ref,
                   g_ref, up_ref, dn_ref, fg_ref, o_ref, *, final):
    y = yc_ref[...] + d_ref[...] * u_ref[...]
    y = jax.nn.gelu(y)
    gate = jnp.dot(y.astype(BF16), gw_ref[...], preferred_element_type=F32) + gb_ref[...]
    y_ssm = (y * jax.nn.sigmoid(gate)).astype(BF16)
    o1 = SSM_WIDTH + POOL_WIDTH
    mix = jnp.dot(y_ssm, wo_ref[0:SSM_WIDTH, :], preferred_element_type=F32)
    mix = mix + jnp.dot(yp_ref[...], wo_ref[SSM_WIDTH:o1, :], preferred_element_type=F32)
    mix = mix + jnp.dot(ym_ref[...], wo_ref[o1:, :], preferred_element_type=F32)
    x1 = x_ref[...] + mix
    hn = _rms(x1, g_ref[...]).astype(BF16)
    acc = jnp.zeros_like(x1)
    for c in range(0, D_FF, FF_CHUNK):
        hc = jnp.dot(hn, up_ref[:, c:c + FF_CHUNK], preferred_element_type=F32)
        hc = jnp.square(jnp.maximum(hc, 0.0)).astype(BF16)
        acc = acc + jnp.dot(hc, dn_ref[c:c + FF_CHUNK, :], preferred_element_type=F32)
    x2 = x1 + acc
    if final:
        x2 = _rms(x2, fg_ref[...])
    o_ref[...] = x2


def _outmlp(x2d, u, yc, yp, ym, layer, d, gw, gb, wo, g, up, dn, fg, *, tm, final):
    n = x2d.shape[0]
    row = lambda i: (i, 0)
    vmem = (4 * tm * D_MODEL * 4 + 2 * tm * (2 * 256 * 4 + 256 * 2 + MLA_WIDTH * 2)
            + _layer_bytes(gw, wo, up, dn)
            + 6 * tm * D_MODEL * 4 + 3 * tm * FF_CHUNK * 4)
    return pl.pallas_call(
        functools.partial(_outmlp_kernel, final=final),
        grid=(n // tm,),
        in_specs=[
            pl.BlockSpec((tm, D_MODEL), row), pl.BlockSpec((tm, SSM_WIDTH), row),
            pl.BlockSpec((tm, SSM_WIDTH), row), pl.BlockSpec((tm, POOL_WIDTH), row),
            pl.BlockSpec((tm, MLA_WIDTH), row),
            _layer_spec(d.shape, layer), _layer_spec(gw.shape, layer), _layer_spec(gb.shape, layer),
            _layer_spec(wo.shape, layer), _layer_spec(g.shape, layer), _layer_spec(up.shape, layer),
            _layer_spec(dn.shape, layer), _const_spec(fg.shape),
        ],
        out_specs=pl.BlockSpec((tm, D_MODEL), row),
        out_shape=jax.ShapeDtypeStruct((n, D_MODEL), F32),
        compiler_params=pltpu.CompilerParams(
            dimension_semantics=("parallel",), vmem_limit_bytes=_vmem_limit(vmem)),
        name="outproj_mlp",
    )(x2d, u, yc, yp, ym, d, gw, gb, wo, g, up, dn, fg)


def _rope_tables(seq):
    half = MLA_ROPE_DIM // 2
    inv_freq = ROPE_BASE ** (-jnp.arange(0, MLA_ROPE_DIM, 2, dtype=F32) / MLA_ROPE_DIM)
    ang = jnp.arange(seq, dtype=F32)[:, None] * inv_freq[None, :]
    cos, sin = jnp.cos(ang), jnp.sin(ang)
    z = jnp.zeros_like(cos)
    ones = jnp.ones((seq, MLA_NOPE_DIM), F32)
    pad = jnp.zeros((seq, HEAD_PAD - MLA_QK_DIM), F32)
    zn = jnp.zeros((seq, MLA_NOPE_DIM), F32)
    cos_t = jnp.concatenate([ones, cos, cos, pad], axis=-1)
    sa_t = jnp.concatenate([zn, -sin, z, pad], axis=-1)
    sb_t = jnp.concatenate([zn, z, sin, pad], axis=-1)
    return cos_t, sa_t, sb_t


def _s5_matrices(a_re, a_im, log_dt, b_re, b_im, c_re, c_im):
    hp = lax.Precision.HIGHEST
    t = S5_CHUNK
    g, p, hch = SSM_GROUPS, SSM_STATE, SSM_GROUP
    dt = jnp.exp(log_dt)[..., None]
    mag = jnp.exp(a_re * dt)
    ang = a_im * dt
    lb_re = mag * jnp.cos(ang)
    lb_im = mag * jnp.sin(ang)
    den = a_re * a_re + a_im * a_im
    f_re = ((lb_re - 1.0) * a_re + lb_im * a_im) / den
    f_im = (lb_im * a_re - (lb_re - 1.0) * a_im) / den
    pw_re = [jnp.ones_like(lb_re)]
    pw_im = [jnp.zeros_like(lb_im)]
    for _ in range(t):
        r, i = pw_re[-1], pw_im[-1]
        pw_re.append(r * lb_re - i * lb_im)
        pw_im.append(r * lb_im + i * lb_re)
    pw_re = jnp.stack(pw_re)
    pw_im = jnp.stack(pw_im)
    bt_re = jnp.swapaxes(b_re, 1, 2)
    bt_im = jnp.swapaxes(b_im, 1, 2)
    fr, fi = f_re[:, :, None, :], f_im[:, :, None, :]
    bb_re = fr * bt_re[None] - fi * bt_im[None]
    bb_im = fr * bt_im[None] + fi * bt_re[None]
    pr, pi = pw_re[:, :, :, None, :], pw_im[:, :, :, None, :]
    w_re = pr * bb_re[None] - pi * bb_im[None]
    w_im = pr * bb_im[None] + pi * bb_re[None]

    def strip(w):
        return jnp.concatenate([w[t - 1:0:-1, 1], (w[0, 0] + w[0, 1])[None], w[1:t, 0]], axis=0)

    lagk = (jnp.einsum('gkp,lghp->ghlk', c_re, strip(w_re), precision=hp)
            - jnp.einsum('gkp,lghp->ghlk', c_im, strip(w_im), precision=hp))
    lagk = lagk.reshape(g, hch, (2 * t - 1) * hch)
    kt = jnp.stack([lagk[:, :, (t - 1 - j) * hch:(t - 1 - j) * hch + t * hch] for j in range(t)],
                   axis=1)
    kt = kt.reshape(S5_PAIRS, 2, MXU_DIM, MXU_DIM)

    def pair_blocks(parts):
        x = jnp.stack(parts, axis=3).reshape(S5_PAIRS, 2, t, hch, 4, p)
        even = jnp.pad(x[:, 0], [(0, 0)] * 4 + [(0, p)])
        odd = jnp.pad(x[:, 1], [(0, 0)] * 4 + [(p, 0)])
        return jnp.stack([even, odd], axis=1).reshape(S5_PAIRS, S5_PAIR_COLS, 4 * S5_PAIR)

    def by_group(w):
        return jnp.swapaxes(w, 0, 1)

    winj = pair_blocks([by_group(w_re[t - 1::-1, 0][:t]), by_group(w_im[t - 1::-1, 0][:t]),
                        by_group(w_re[:t, 1]), by_group(w_im[:t, 1])])

    def c_times(qr, qi):
        qr, qi = jnp.swapaxes(qr, 0, 1)[:, :, None, :], jnp.swapaxes(qi, 0, 1)[:, :, None, :]
        cr, ci = c_re[:, None], c_im[:, None]
        return cr * qr - ci * qi, -(cr * qi + ci * qr)

    of_re, of_im = c_times(pw_re[1:t + 1, 0], pw_im[1:t + 1, 0])
    ob_re, ob_im = c_times(pw_re[t:0:-1, 1], pw_im[t:0:-1, 1])
    wout = jnp.swapaxes(pair_blocks([of_re, of_im, ob_re, ob_im]), 1, 2)
    lam16 = jnp.stack([pw_re[t, 0], pw_im[t, 0], pw_re[t, 1], pw_im[t, 1]])
    lam16 = jnp.transpose(lam16.reshape(4, S5_PAIRS, S5_PAIR), (1, 0, 2))
    return kt.astype(BF16), winj.astype(BF16), wout.astype(BF16), lam16


def _pad_cols(w, lo, hi):
    return jnp.pad(w, [(0, 0)] * (w.ndim - 1) + [(lo, hi)])


def _trunk(x, prm, depth):
    batch, seq, _ = x.shape
    n = batch * seq
    x2d = x.reshape(n, D_MODEL)
    cos_t, sa_t, sb_t = _rope_tables(seq)
    tm = min(OUTMLP_ROWS, seq)
    nseq = math.gcd(batch, max(1, min(SUBLANES, S5_TOKENS_PER_STEP // seq)))
    tq = min(seq, ATTN_SCORE_BYTES // (4 * seq))
    heads = max(1, min(MLA_HEADS, ATTN_KV_BYTES // (4 * seq * HEAD_PAD * 2)))
    perm = _granule_swap_matrix()
    w = prm
    for l in range(depth):
        u, p, q, k, v = _inproj(x2d, seq, l, w['g_mix'], w['win'], w['qg'], w['wuq'], w['kvg'],
                                w['wuk'], w['wuv'], cos_t, sa_t, sb_t,
                                tm=min(INPROJ_ROWS, seq), sub=min(INPROJ_SUB_ROWS, seq))
        yc = _s5(u, seq, l, perm, w['kt'], w['winj'], w['wout'], w['lam16'], nseq=nseq)
        yp = _pool(p, seq, l, w['pool_w'], w['pool_scale'], tile=min(256, seq))
        ym = _attention(q, k, v, seq, tq=tq, heads=heads)
        x2d = _outmlp(x2d, u, yc, yp, ym, l, w['ssm_d'], w['glu_w'], w['glu_b'], w['w_out'],
                      w['g_mlp'], w['up'], w['dn'], prm['final_g'], tm=tm,
                      final=(l == depth - 1))
    return x2d.reshape(batch, seq, D_MODEL)


def _prepare(norm_mix_g, w_in, ssm_a_re, ssm_a_im, ssm_log_dt, ssm_b_re, ssm_b_im, ssm_c_re, ssm_c_im,
             ssm_d, ssm_glu_w, ssm_glu_b, pool_w, pool_scale, mla_q_norm_g, mla_w_uq, mla_kv_norm_g,
             mla_w_ukv, w_out, norm_mlp_g, mlp_w_up, mlp_w_down, final_norm_g):
    depth = w_in.shape[0]
    o3 = SSM_WIDTH + POOL_WIDTH + MLA_Q_LORA + MLA_KV_LORA
    win = jnp.concatenate(
        [w_in[..., :o3], _pad_cols(w_in[..., o3:], MLA_NOPE_DIM, HEAD_PAD - MLA_QK_DIM)], axis=-1)
    wuq = _pad_cols(mla_w_uq.reshape(depth, MLA_Q_LORA, MLA_HEADS, MLA_QK_DIM), 0,
                    HEAD_PAD - MLA_QK_DIM).reshape(depth, MLA_Q_LORA, QK_PAD)
    wukv = mla_w_ukv.reshape(depth, MLA_KV_LORA, MLA_HEADS, MLA_NOPE_DIM + MLA_V_DIM)
    wuk = _pad_cols(wukv[..., :MLA_NOPE_DIM], 0, HEAD_PAD - MLA_NOPE_DIM).reshape(
        depth, MLA_KV_LORA, QK_PAD)
    wuv = _pad_cols(wukv[..., MLA_NOPE_DIM:], 0, HEAD_PAD - MLA_V_DIM).reshape(
        depth, MLA_KV_LORA, QK_PAD)
    kt, winj, wout, lam16 = jax.vmap(_s5_matrices)(
        ssm_a_re, ssm_a_im, ssm_log_dt, ssm_b_re, ssm_b_im, ssm_c_re, ssm_c_im)
    nwin = len(POOL_WINDOWS)
    eye = jnp.eye(nwin, dtype=F32)
    pool_bd = jnp.einsum('ab,lacd->lacbd', eye, pool_w).reshape(depth, POOL_WIDTH, POOL_WIDTH)
    prm = dict(
        g_mix=norm_mix_g[:, None, :], win=win.astype(BF16),
        qg=mla_q_norm_g[:, None, :], wuq=wuq.astype(BF16),
        kvg=mla_kv_norm_g[:, None, :], wuk=wuk.astype(BF16), wuv=wuv.astype(BF16),
        kt=kt, winj=winj, wout=wout, lam16=lam16,
        pool_w=pool_bd.astype(BF16), pool_scale=pool_scale[:, None, :],
        ssm_d=ssm_d[:, None, :], glu_w=ssm_glu_w.astype(BF16), glu_b=ssm_glu_b[:, None, :],
        w_out=w_out.astype(BF16), g_mlp=norm_mlp_g[:, None, :],
        up=mlp_w_up.astype(BF16), dn=mlp_w_down.astype(BF16),
        final_g=final_norm_g[None, :],
    )
    return prm


def kernel(x_prompt, x_sample, norm_mix_g, w_in, ssm_a_re, ssm_a_im, ssm_log_dt, ssm_b_re, ssm_b_im, ssm_c_re, ssm_c_im, ssm_d, ssm_glu_w, ssm_glu_b, pool_w, pool_scale, mla_q_norm_g, mla_w_uq, mla_kv_norm_g, mla_w_ukv, w_out, norm_mlp_g, mlp_w_up, mlp_w_down, final_norm_g):
    depth = w_in.shape[0]
    prm = _prepare(norm_mix_g, w_in, ssm_a_re, ssm_a_im, ssm_log_dt, ssm_b_re, ssm_b_im, ssm_c_re,
                   ssm_c_im, ssm_d, ssm_glu_w, ssm_glu_b, pool_w, pool_scale, mla_q_norm_g, mla_w_uq,
                   mla_kv_norm_g, mla_w_ukv, w_out, norm_mlp_g, mlp_w_up, mlp_w_down, final_norm_g)
    return (_trunk(x_prompt, prm, depth), _trunk(x_sample, prm, depth))
```
